```python
import jax, jax.numpy as jnp
from jax import lax
import numpy as np

D_MODEL = 1024
BATCH = 8
SEQ = 2048
DEPTH = 2
DEC_BATCH = 128
DEC_SEQ = 4
PAST_LEN = 16384
PAGE_SIZE = 128

N_MIXERS = 2
POOL_WINDOWS = (2, 4, 8, 16)
POOL_GROUPS = len(POOL_WINDOWS)
POOL_GROUP_DIM = D_MODEL // POOL_GROUPS
POOL_CTX = max(POOL_WINDOWS) - 1
N_HEADS = 8
HEAD_K = 128
HEAD_V = D_MODEL // N_HEADS
F_DIM = N_HEADS * HEAD_K
V_DIM = N_HEADS * HEAD_V
GLA_CHUNK = 64
D_FF = 2816
CONV_W = 3
N_POOL = (DEPTH + 1) // 2
N_HGRN = DEPTH // 2
EPS = 1e-6

kernel_name = "hybrid_pool_hgrn2_convffn_step"


def rmsnorm(x, g):
    xf = x.astype(jnp.float32)
    r = lax.rsqrt(jnp.mean(xf * xf, axis=-1, keepdims=True) + EPS)
    return (xf * r * g.astype(jnp.float32)).astype(x.dtype)


def pool_mixer(h, ctx, pos0, w_pool, scale):
    B, L, D = h.shape
    full = jnp.concatenate([ctx.astype(h.dtype), h], axis=1)
    csum = jnp.cumsum(full.astype(jnp.float32), axis=1)
    csum = jnp.concatenate([jnp.zeros((B, 1, D), jnp.float32), csum], axis=1)
    hi = POOL_CTX + 1 + np.arange(L)
    hf = h.astype(jnp.float32)
    outs = []
    for g, w in enumerate(POOL_WINDOWS):
        lo_c, hi_c = g * POOL_GROUP_DIM, (g + 1) * POOL_GROUP_DIM
        cnt = np.minimum(pos0 + np.arange(L) + 1, w).astype(np.float32)
        s = csum[:, hi, lo_c:hi_c] - csum[:, hi - w, lo_c:hi_c]
        outs.append(s / cnt[None, :, None] - hf[..., lo_c:hi_c])
    p = jnp.stack(outs, axis=2)
    y = jnp.einsum('blgd,gde->blge', p, w_pool.astype(jnp.float32)).reshape(B, L, D)
    y = y * scale.astype(jnp.float32)
    return y.astype(h.dtype), full[:, -POOL_CTX:]


def gla_chunked(q, k, v, log_f, s0, chunk):
    B, L, H, DK = q.shape
    DV = v.shape[-1]
    n = -(-L // chunk)
    pad = n * chunk - L

    def prep(a):
        a = jnp.pad(a, ((0, 0), (0, pad), (0, 0), (0, 0)))
        return a.reshape(B, n, chunk, H, a.shape[-1]).transpose(1, 0, 3, 2, 4)

    qs, ks, vs, gs = prep(q), prep(k), prep(v), prep(log_f)
    causal = jnp.tril(jnp.ones((chunk, chunk), bool))[:, :, None]

    def step(S, xs):
        qc, kc, vc, gc = xs
        b = jnp.cumsum(gc, axis=2)
        o_inter = jnp.einsum('bhtd,bhde->bhte', qc * jnp.exp(b), S)
        diff = b[:, :, :, None, :] - b[:, :, None, :, :]
        decay = jnp.where(causal, jnp.exp(jnp.where(causal, diff, 0.0)), 0.0)
        a = jnp.einsum('bhtd,bhsd,bhtsd->bhts', qc, kc, decay)
        o = o_inter + jnp.einsum('bhts,bhse->bhte', a, vc)
        b_last = b[:, :, -1:, :]
        S = jnp.exp(b_last[:, :, 0, :])[..., None] * S + jnp.einsum(
            'bhsd,bhse->bhde', kc * jnp.exp(b_last - b), vc)
        return S, o

    S, o = lax.scan(step, s0, (qs, ks, vs, gs))
    o = o.transpose(1, 0, 3, 2, 4).reshape(B, n * chunk, H, DV)[:, :L]
    return o, S


def hgrn_mixer(h, s0, w_in, lb, gnorm, w_out):
    B, L, _ = h.shape
    proj = h @ w_in
    q, f, i, g = jnp.split(proj, [F_DIM, 2 * F_DIM, 2 * F_DIM + V_DIM], axis=-1)
    q = jax.nn.silu(q.astype(jnp.float32)) * (HEAD_K ** -0.5)
    f = f.astype(jnp.float32)
    lbf = lb.astype(jnp.float32)
    log_f = jnp.logaddexp(jnp.log(lbf), jnp.log1p(-lbf) + jax.nn.log_sigmoid(f))
    k = (1.0 - lbf) * jax.nn.sigmoid(-f)
    hq = q.reshape(B, L, N_HEADS, HEAD_K)
    hk = k.reshape(B, L, N_HEADS, HEAD_K)
    hf = log_f.reshape(B, L, N_HEADS, HEAD_K)
    hv = i.astype(jnp.float32).reshape(B, L, N_HEADS, HEAD_V)
    o, s_new = gla_chunked(hq, hk, hv, hf, s0.astype(jnp.float32), min(GLA_CHUNK, L))
    o = rmsnorm(o, gnorm) * jax.nn.silu(g.astype(jnp.float32).reshape(B, L, N_HEADS, HEAD_V))
    y = o.reshape(B, L, V_DIM).astype(h.dtype) @ w_out
    return y, s_new.astype(s0.dtype)


def conv_ffn(h, ctx, w_up, conv_w, conv_b, w_down):
    L = h.shape[1]
    u = h @ w_up
    full = jnp.concatenate([ctx.astype(u.dtype), u], axis=1)
    c = conv_b + sum(full[:, j:j + L] * conv_w[j] for j in range(CONV_W))
    gate, val = jnp.split(c, 2, axis=-1)
    y = (jax.nn.gelu(gate, approximate=True) * val) @ w_down
    return y, full[:, -(CONV_W - 1):]


def trunk(x, pos0, pool_ctx, hgrn_s, ffn_ctx, norm_mix_pre, norm_mix_post, norm_ffn_pre,
          norm_ffn_post, pool_w, pool_scale, hgrn_w_in, hgrn_lb_logits, hgrn_gnorm, hgrn_w_out,
          ffn_w_up, ffn_conv_w, ffn_conv_b, ffn_w_down):
    lb_all = jnp.cumsum(jax.nn.softmax(hgrn_lb_logits.astype(jnp.float32), axis=0), axis=0)
    lb_all = lb_all - lb_all[0:1]
    new_pool, new_hgrn, new_ffn = [], [], []
    for li in range(DEPTH):
        j = li // N_MIXERS
        h = rmsnorm(x, norm_mix_pre[li])
        if li % N_MIXERS == 0:
            m, st = pool_mixer(h, pool_ctx[j], pos0, pool_w[j], pool_scale[j])
            new_pool.append(st)
        else:
            m, st = hgrn_mixer(h, hgrn_s[j], hgrn_w_in[j], lb_all[li], hgrn_gnorm[j], hgrn_w_out[j])
            new_hgrn.append(st)
        x = x + rmsnorm(m, norm_mix_post[li])
        h = rmsnorm(x, norm_ffn_pre[li])
        m, st = conv_ffn(h, ffn_ctx[li], ffn_w_up[li], ffn_conv_w[li], ffn_conv_b[li], ffn_w_down[li])
        new_ffn.append(st)
        x = x + rmsnorm(m, norm_ffn_post[li])
    return x, jnp.stack(new_pool), jnp.stack(new_hgrn), jnp.stack(new_ffn)


def setup_inputs(seed: int = 0) -> dict:
    key = jax.random.key(seed)
    ks = jax.random.split(key, 24)
    f32 = jnp.float32
    nrm = lambda k, shape, s: jax.random.normal(k, shape, f32) * s
    return {
        "x_prompt": nrm(ks[0], (BATCH, SEQ, D_MODEL), 1.0),
        "x_sample": nrm(ks[1], (DEC_BATCH, DEC_SEQ, D_MODEL), 1.0),
        "state_pool": nrm(ks[2], (N_POOL, DEC_BATCH, POOL_CTX, D_MODEL), 1.0),
        "state_hgrn": nrm(ks[3], (N_HGRN, DEC_BATCH, N_HEADS, HEAD_K, HEAD_V), 0.5),
        "state_ffn_conv": nrm(ks[4], (DEPTH, DEC_BATCH, CONV_W - 1, 2 * D_FF), 1.0),
        "norm_mix_pre": 1.0 + nrm(ks[5], (DEPTH, D_MODEL), 0.05),
        "norm_mix_post": 1.0 + nrm(ks[6], (DEPTH, D_MODEL), 0.05),
        "norm_ffn_pre": 1.0 + nrm(ks[7], (DEPTH, D_MODEL), 0.05),
        "norm_ffn_post": 1.0 + nrm(ks[8], (DEPTH, D_MODEL), 0.05),
        "pool_w": nrm(ks[9], (N_POOL, POOL_GROUPS, POOL_GROUP_DIM, POOL_GROUP_DIM), POOL_GROUP_DIM ** -0.5),
        "pool_scale": 1.0 + nrm(ks[10], (N_POOL, D_MODEL), 0.1),
        "hgrn_w_in": nrm(ks[11], (N_HGRN, D_MODEL, 2 * F_DIM + 2 * V_DIM), D_MODEL ** -0.5),
        "hgrn_lb_logits": nrm(ks[12], (DEPTH, F_DIM), 0.5),
        "hgrn_gnorm": 1.0 + nrm(ks[13], (N_HGRN, HEAD_V), 0.05),
        "hgrn_w_out": nrm(ks[14], (N_HGRN, V_DIM, D_MODEL), V_DIM ** -0.5),
        "ffn_w_up": nrm(ks[15], (DEPTH, D_MODEL, 2 * D_FF), D_MODEL ** -0.5),
        "ffn_conv_w": nrm(ks[16], (DEPTH, CONV_W, 2 * D_FF), 0.5),
        "ffn_conv_b": nrm(ks[17], (DEPTH, 2 * D_FF), 0.02),
        "ffn_w_down": nrm(ks[18], (DEPTH, D_FF, D_MODEL), D_FF ** -0.5),
    }


def reference(x_prompt, x_sample, state_pool, state_hgrn, state_ffn_conv, norm_mix_pre,
              norm_mix_post, norm_ffn_pre, norm_ffn_post, pool_w, pool_scale, hgrn_w_in,
              hgrn_lb_logits, hgrn_gnorm, hgrn_w_out, ffn_w_up, ffn_conv_w, ffn_conv_b, ffn_w_down):
    weights = (norm_mix_pre, norm_mix_post, norm_ffn_pre, norm_ffn_post, pool_w, pool_scale,
               hgrn_w_in, hgrn_lb_logits, hgrn_gnorm, hgrn_w_out, ffn_w_up, ffn_conv_w,
               ffn_conv_b, ffn_w_down)
    dt = x_prompt.dtype
    pool0 = jnp.zeros((N_POOL, BATCH, POOL_CTX, D_MODEL), dt)
    hgrn0 = jnp.zeros((N_HGRN, BATCH, N_HEADS, HEAD_K, HEAD_V), dt)
    ffn0 = jnp.zeros((DEPTH, BATCH, CONV_W - 1, 2 * D_FF), dt)
    y_prompt, pool_p, hgrn_p, ffn_p = trunk(x_prompt, 0, pool0, hgrn0, ffn0, *weights)
    y_sample, pool_s, hgrn_s, ffn_s = trunk(x_sample, PAST_LEN, state_pool, state_hgrn,
                                            state_ffn_conv, *weights)
    return (y_prompt, y_sample, pool_p, pool_s, hgrn_p, hgrn_s, ffn_p, ffn_s)
```

```python
import functools

import jax
import jax.numpy as jnp
import numpy as np
from jax import lax
from jax.experimental import pallas as pl
from jax.experimental.pallas import tpu as pltpu

D_MODEL = 1024
POOL_WINDOWS = (2, 4, 8, 16)
POOL_GROUP_DIM = D_MODEL // len(POOL_WINDOWS)
POOL_CTX = max(POOL_WINDOWS) - 1
N_HEADS = 8
HEAD_K = 128
HEAD_V = D_MODEL // N_HEADS
F_DIM = N_HEADS * HEAD_K
V_DIM = N_HEADS * HEAD_V
D_FF = 2816
CONV_W = 3
EPS = 1e-6

SUBLANES = 8
ROW_TILE = 512
FF_CHUNK = 256
GLA_CHUNK = 64
GLA_LEVELS = 6
GLA_TILE = 256
SAMPLE_T_PAD = 8
SAMPLE_B_BLK = 8
VMEM_LIMIT = 56 * 1024 * 1024

_F32 = jnp.float32
_BF16 = jnp.bfloat16


def _rms(x, g):
    ms = jnp.mean(x * x, axis=-1, keepdims=True)
    return x * lax.rsqrt(ms + EPS) * g


def _dot(a, b):
    return jnp.dot(a, b, preferred_element_type=_F32)


def _dot_nt(a, b):
    return lax.dot_general(a, b, (((1,), (1,)), ((), ())), preferred_element_type=_F32)


def _dot_tn(a, b):
    return lax.dot_general(a, b, (((0,), (0,)), ((), ())), preferred_element_type=_F32)


def _const_spec(shape):
    nd = len(shape)
    return pl.BlockSpec(shape, lambda *_: (0,) * nd, pipeline_mode=pl.Buffered(1))


def _params():
    return pltpu.CompilerParams(
        dimension_semantics=("arbitrary", "arbitrary"), vmem_limit_bytes=VMEM_LIMIT)


def _pool_kernel(x_ref, ctx_ref, gpre_ref, w_ref, scale_ref, gpost_ref, y_ref, st_ref, hbuf,
                 *, rows, step, pos0, n_tiles):
    halo = (POOL_CTX + 1) * step
    l = pl.program_id(1)

    @pl.when(l == 0)
    def _():
        hbuf[0:step, :] = jnp.zeros((step, D_MODEL), _F32)
        hbuf[step:halo, :] = ctx_ref[0]

    x = x_ref[0]
    h = _rms(x, gpre_ref[...])
    hbuf[halo:halo + rows, :] = h

    if pos0 < POOL_CTX:
        assert step == 1
        pos = pos0 + l * rows + lax.broadcasted_iota(jnp.int32, (rows, 1), 0)
    outs = []
    for g, w in enumerate(POOL_WINDOWS):
        cs = slice(g * POOL_GROUP_DIM, (g + 1) * POOL_GROUP_DIM)
        s = h[:, cs]
        for j in range(1, w):
            s = s + hbuf[halo - j * step:halo - j * step + rows, cs]
        if pos0 < POOL_CTX:
            cnt = jnp.minimum(pos + 1, w).astype(_F32)
        else:
            cnt = float(w)
        p = s / cnt - h[:, cs]
        outs.append(_dot(p.astype(_BF16), w_ref[g]))
    m = jnp.concatenate(outs, axis=-1) * scale_ref[...]
    y_ref[0] = x + _rms(m, gpost_ref[...])

    @pl.when(l == n_tiles - 1)
    def _():
        st_ref[0] = hbuf[rows + step:rows + halo, :]

    if n_tiles > 1:
        hbuf[0:halo, :] = hbuf[rows:rows + halo, :]


def _pool_layer(x, ctx, gpre, w_bf16, scale, gpost, *, rows, step, pos0):
    nb, n, _ = x.shape
    n_tiles = n // rows
    halo = (POOL_CTX + 1) * step
    ctx_rows = POOL_CTX * step
    kern = functools.partial(_pool_kernel, rows=rows, step=step, pos0=pos0, n_tiles=n_tiles)
    return pl.pallas_call(
        kern,
        grid=(nb, n_tiles),
        in_specs=[
            pl.BlockSpec((1, rows, D_MODEL), lambda b, l: (b, l, 0)),
            pl.BlockSpec((1, ctx_rows, D_MODEL), lambda b, l: (b, 0, 0)),
            _const_spec((1, D_MODEL)),
            _const_spec(w_bf16.shape),
            _const_spec((1, D_MODEL)),
            _const_spec((1, D_MODEL)),
        ],
        out_specs=[
            pl.BlockSpec((1, rows, D_MODEL), lambda b, l: (b, l, 0)),
            pl.BlockSpec((1, ctx_rows, D_MODEL), lambda b, l: (b, 0, 0)),
        ],
        out_shape=[
            jax.ShapeDtypeStruct(x.shape, _F32),
            jax.ShapeDtypeStruct((nb, ctx_rows, D_MODEL), _F32),
        ],
        scratch_shapes=[pltpu.VMEM((halo + rows, D_MODEL), _F32)],
        compiler_params=_params(),
        name="pool_mixer",
    )(x, ctx, gpre, w_bf16, scale, gpost)


def _ffn_kernel(x_ref, ctx_ref, gpre_ref, wup_ref, cw_ref, cb_ref, wdn_ref, gpost_ref,
                y_ref, st_ref, carry, gbuf, vbuf, abuf, *, rows, step, n_tiles):
    halo = max(SUBLANES, (CONV_W - 1) * step)
    l = pl.program_id(1)

    @pl.when(l == 0)
    def _():
        if halo > (CONV_W - 1) * step:
            carry[0:halo - (CONV_W - 1) * step, :] = jnp.zeros(
                (halo - (CONV_W - 1) * step, 2 * D_FF), _F32)
        carry[halo - (CONV_W - 1) * step:halo, :] = ctx_ref[0]

    x = x_ref[0]
    h = _rms(x, gpre_ref[...]).astype(_BF16)
    for c in range(D_FF // FF_CHUNK):
        conv = []
        for buf, off in ((gbuf, 0), (vbuf, D_FF)):
            cs = slice(off + c * FF_CHUNK, off + (c + 1) * FF_CHUNK)
            u = _dot(h, wup_ref[:, cs])
            buf[0:halo, :] = carry[:, cs]
            buf[halo:halo + rows, :] = u
            carry[:, cs] = buf[rows:rows + halo, :]
            conv.append(cb_ref[:, cs]
                        + buf[halo - 2 * step:halo - 2 * step + rows, :] * cw_ref[0:1, cs]
                        + buf[halo - step:halo - step + rows, :] * cw_ref[1:2, cs]
                        + u * cw_ref[2:3, cs])
        a = jax.nn.gelu(conv[0], approximate=True) * conv[1]
        abuf[:, c * FF_CHUNK:(c + 1) * FF_CHUNK] = a.astype(_BF16)
    m = _dot(abuf[...], wdn_ref[...])
    y_ref[0] = x + _rms(m, gpost_ref[...])

    @pl.when(l == n_tiles - 1)
    def _():
        st_ref[0] = carry[halo - (CONV_W - 1) * step:halo, :]


def _ffn_layer(x, ctx, gpre, wup_bf16, cw, cb, wdn_bf16, gpost, *, rows, step):
    nb, n, _ = x.shape
    n_tiles = n // rows
    halo = max(SUBLANES, (CONV_W - 1) * step)
    ctx_rows = (CONV_W - 1) * step
    kern = functools.partial(_ffn_kernel, rows=rows, step=step, n_tiles=n_tiles)
    return pl.pallas_call(
        kern,
        grid=(nb, n_tiles),
        in_specs=[
            pl.BlockSpec((1, rows, D_MODEL), lambda b, l: (b, l, 0)),
            pl.BlockSpec((1, ctx_rows, 2 * D_FF), lambda b, l: (b, 0, 0)),
            _const_spec((1, D_MODEL)),
            _const_spec((D_MODEL, 2 * D_FF)),
            _const_spec((CONV_W, 2 * D_FF)),
            _const_spec((1, 2 * D_FF)),
            _const_spec((D_FF, D_MODEL)),
            _const_spec((1, D_MODEL)),
        ],
        out_specs=[
            pl.BlockSpec((1, rows, D_MODEL), lambda b, l: (b, l, 0)),
            pl.BlockSpec((1, ctx_rows, 2 * D_FF), lambda b, l: (b, 0, 0)),
        ],
        out_shape=[
            jax.ShapeDtypeStruct(x.shape, _F32),
            jax.ShapeDtypeStruct((nb, ctx_rows, 2 * D_FF), _F32),
        ],
        scratch_shapes=[
            pltpu.VMEM((halo, 2 * D_FF), _F32),
            pltpu.VMEM((halo + rows, FF_CHUNK), _F32),
            pltpu.VMEM((halo + rows, FF_CHUNK), _F32),
            pltpu.VMEM((rows, D_FF), _BF16),
        ],
        compiler_params=_params(),
        name="conv_ffn",
    )(x, ctx, gpre, wup_bf16, cw, cb, wdn_bf16, gpost)


def _hproj_kernel(x_ref, gpre_ref, win_ref, lbl_ref, q_ref, k_ref, lf_ref, v_ref, gs_ref):
    h = _rms(x_ref[...], gpre_ref[...]).astype(_BF16)
    l0 = lbl_ref[0:1, :]
    l1 = lbl_ref[1:2, :]
    mx = jnp.maximum(l0, l1)
    e0 = jnp.exp(l0 - mx)
    e1 = jnp.exp(l1 - mx)
    lb = e1 / (e0 + e1)

    qr = _dot(h, win_ref[:, 0:F_DIM])
    q_ref[...] = qr * jax.nn.sigmoid(qr) * (HEAD_K ** -0.5)
    fr = _dot(h, win_ref[:, F_DIM:2 * F_DIM])
    lf_ref[...] = jnp.log(lb + (1.0 - lb) * jax.nn.sigmoid(fr))
    k_ref[...] = (1.0 - lb) * jax.nn.sigmoid(-fr)
    v_ref[...] = _dot(h, win_ref[:, 2 * F_DIM:2 * F_DIM + V_DIM])
    gr = _dot(h, win_ref[:, 2 * F_DIM + V_DIM:2 * F_DIM + 2 * V_DIM])
    gs_ref[...] = gr * jax.nn.sigmoid(gr)


def _hgrn_proj(x2d, gpre, win_bf16, lb_logits, *, rows):
    n = x2d.shape[0]
    row_spec = pl.BlockSpec((rows, D_MODEL), lambda i: (i, 0))
    out = jax.ShapeDtypeStruct((n, D_MODEL), _F32)
    return pl.pallas_call(
        _hproj_kernel,
        grid=(n // rows,),
        in_specs=[row_spec, _const_spec((1, D_MODEL)), _const_spec(win_bf16.shape),
                  _const_spec(lb_logits.shape)],
        out_specs=[row_spec] * 5,
        out_shape=[out] * 5,
        compiler_params=pltpu.CompilerParams(
            dimension_semantics=("arbitrary",), vmem_limit_bytes=VMEM_LIMIT),
        name="hgrn_proj",
    )(x2d, gpre, win_bf16, lb_logits)


def _hout_kernel(x_ref, o_ref, gs_ref, gn_ref, wout_ref, gpost_ref, y_ref, zbuf):
    for hd in range(N_HEADS):
        hs = slice(hd * HEAD_V, (hd + 1) * HEAD_V)
        zbuf[:, hs] = (_rms(o_ref[:, hs], gn_ref[...]) * gs_ref[:, hs]).astype(_BF16)
    m = _dot(zbuf[...], wout_ref[...])
    y_ref[...] = x_ref[...] + _rms(m, gpost_ref[...])


def _hgrn_out(x2d, o2d, gs2d, gnorm, wout_bf16, gpost, *, rows):
    n = x2d.shape[0]
    row_spec = pl.BlockSpec((rows, D_MODEL), lambda i: (i, 0))
    return pl.pallas_call(
        _hout_kernel,
        grid=(n // rows,),
        in_specs=[row_spec, row_spec, row_spec, _const_spec((1, HEAD_V)),
                  _const_spec(wout_bf16.shape), _const_spec((1, D_MODEL))],
        out_specs=row_spec,
        out_shape=jax.ShapeDtypeStruct((n, D_MODEL), _F32),
        scratch_shapes=[pltpu.VMEM((rows, V_DIM), _BF16)],
        compiler_params=pltpu.CompilerParams(
            dimension_semantics=("arbitrary",), vmem_limit_bytes=VMEM_LIMIT),
        name="hgrn_out",
    )(x2d, o2d, gs2d, gnorm, wout_bf16, gpost)


def _gla_exponent_matrix():
    c = GLA_CHUNK
    t = np.arange(c)[:, None]
    w = np.arange(c)[None, :]
    blocks = [(w <= t), (w > t)]
    for lev in range(1, GLA_LEVELS + 1):
        n = 1 << lev
        mid = (t // n) * n + n // 2 - 1
        later = (t > mid) & (w > mid) & (w <= t)
        earlier = (t <= mid) & (w > t) & (w <= mid)
        blocks.append(later | earlier)
    return np.concatenate(blocks, axis=0).astype(np.float32)


def _gla_kernel(q_ref, k_ref, lf_ref, v_ref, s0_ref, mall_ref, o_ref, sn_ref, st,
                *, rows, n_tiles):
    c = GLA_CHUNK
    l = pl.program_id(1)

    @pl.when(l == 0)
    def _():
        for hd in range(N_HEADS):
            st[hd] = s0_ref[0, hd].T

    ti = lax.broadcasted_iota(jnp.int32, (c, c), 0)
    si = lax.broadcasted_iota(jnp.int32, (c, c), 1)
    ri = lax.broadcasted_iota(jnp.int32, (c, 1), 0)
    diag = (ti == si).astype(_F32)
    lvl_mask, lvl_role = [], []
    for lev in range(1, GLA_LEVELS + 1):
        same = (ti >> lev) == (si >> lev)
        t_late = ((ti >> (lev - 1)) & 1) == 1
        s_early = ((si >> (lev - 1)) & 1) == 0
        lvl_mask.append((same & t_late & s_early).astype(_F32))
        lvl_role.append(((ri >> (lev - 1)) & 1) == 1)

    def chunk(ci, carry):
        rs = pl.ds(pl.multiple_of(ci * c, c), c)
        lf = lf_ref[0, rs, :]
        ex = jnp.exp(jnp.dot(mall_ref[...], lf, preferred_element_type=_F32,
                             precision=lax.Precision.HIGHEST))
        eb = ex[0:c]
        eb_rest = ex[c:2 * c]
        eb_last = eb[c - 1:c]
        q = q_ref[0, rs, :]
        k = k_ref[0, rs, :]
        v = v_ref[0, rs, :]
        qd = (q * eb).astype(_BF16)
        kd = (k * eb_rest).astype(_BF16)
        for hd in range(N_HEADS):
            hs = slice(hd * HEAD_K, (hd + 1) * HEAD_K)
            s_t = st[hd]
            qh = q[:, hs]
            kh = k[:, hs]
            vh = v[:, hs].astype(_BF16)
            a = diag * _dot_nt(qh.astype(_BF16), kh.astype(_BF16))
            for lev in range(GLA_LEVELS):
                e = ex[(2 + lev) * c:(3 + lev) * c, hs]
                xl = (jnp.where(lvl_role[lev], qh, kh) * e).astype(_BF16)
                a = a + lvl_mask[lev] * _dot_nt(xl, xl)
            o = _dot_nt(qd[:, hs], s_t.astype(_BF16)) + _dot(a.astype(_BF16), vh)
            o_ref[0, rs, hs] = o
            st[hd] = s_t * eb_last[:, hs] + _dot_tn(vh, kd[:, hs])
        return carry

    lax.fori_loop(0, rows // c, chunk, 0)

    @pl.when(l == n_tiles - 1)
    def _():
        for hd in range(N_HEADS):
            sn_ref[0, hd] = st[hd].T


def _gla_prompt(q, k, lf, v, s0, *, rows):
    nb, n, _ = q.shape
    n_tiles = n // rows
    mall = jnp.asarray(_gla_exponent_matrix())
    row_spec = pl.BlockSpec((1, rows, D_MODEL), lambda b, l: (b, l, 0))
    st_spec = pl.BlockSpec((1, N_HEADS, HEAD_K, HEAD_V), lambda b, l: (b, 0, 0, 0))
    kern = functools.partial(_gla_kernel, rows=rows, n_tiles=n_tiles)
    return pl.pallas_call(
        kern,
        grid=(nb, n_tiles),
        in_specs=[row_spec, row_spec, row_spec, row_spec, st_spec, _const_spec(mall.shape)],
        out_specs=[row_spec, st_spec],
        out_shape=[jax.ShapeDtypeStruct(q.shape, _F32), jax.ShapeDtypeStruct(s0.shape, _F32)],
        scratch_shapes=[pltpu.VMEM((N_HEADS, HEAD_V, HEAD_K), _F32)],
        compiler_params=_params(),
        name="gla_prompt",
    )(q, k, lf, v, s0, mall)


def _gla_sample_kernel(q_ref, k_ref, lf_ref, v_ref, s0_ref, o_ref, sn_ref, *, steps):
    tp = SAMPLE_T_PAD
    ri = lax.broadcasted_iota(jnp.int32, (tp, 1), 0)
    ri128 = lax.broadcasted_iota(jnp.int32, (HEAD_K, HEAD_K), 0)
    ci128 = lax.broadcasted_iota(jnp.int32, (HEAD_K, HEAD_K), 1)
    eye = (ri128 == ci128).astype(_F32)
    for bi in range(SAMPLE_B_BLK):
        lf = lf_ref[bi]
        q = q_ref[bi]
        k = k_ref[bi]
        v = v_ref[bi]
        b = jnp.zeros_like(lf)
        for j in range(steps):
            b = b + jnp.where(ri >= j, lf[j:j + 1, :], 0.0)
        eb = jnp.exp(b)
        b_last = b[steps - 1:steps, :]
        eb_last = jnp.exp(b_last)
        qd = (q * eb).astype(_BF16)
        kd = k * jnp.exp(b_last - b)
        o_intra = jnp.zeros_like(q)
        for s in range(steps):
            arg = jnp.where(ri >= s, b - b[s:s + 1, :], 0.0)
            p = jnp.where(ri >= s, q * k[s:s + 1, :] * jnp.exp(arg), 0.0)
            parts = []
            for hd in range(N_HEADS):
                hs = slice(hd * HEAD_K, (hd + 1) * HEAD_K)
                a_s = jnp.sum(p[:, hs], axis=-1, keepdims=True)
                parts.append(a_s * v[s:s + 1, hs])
            o_intra = o_intra + jnp.concatenate(parts, axis=-1)
        for hd in range(N_HEADS):
            hs = slice(hd * HEAD_K, (hd + 1) * HEAD_K)
            s0 = s0_ref[bi, hd]
            o_ref[bi, :, hs] = o_intra[:, hs] + _dot(qd[:, hs], s0.astype(_BF16))
            dcol = jnp.sum(eye * eb_last[:, hs], axis=-1, keepdims=True)
            upd = dcol * s0
            for s in range(steps):
                kcol = jnp.sum(eye * kd[s:s + 1, hs], axis=-1, keepdims=True)
                upd = upd + kcol * v[s:s + 1, hs]
            sn_ref[bi, hd] = upd


def _gla_sample(q, k, lf, v, s0, *, steps):
    nb = q.shape[0]
    row_spec = pl.BlockSpec((SAMPLE_B_BLK, SAMPLE_T_PAD, D_MODEL), lambda i: (i, 0, 0))
    st_spec = pl.BlockSpec((SAMPLE_B_BLK, N_HEADS, HEAD_K, HEAD_V), lambda i: (i, 0, 0, 0))
    kern = functools.partial(_gla_sample_kernel, steps=steps)
    return pl.pallas_call(
        kern,
        grid=(nb // SAMPLE_B_BLK,),
        in_specs=[row_spec, row_spec, row_spec, row_spec, st_spec],
        out_specs=[row_spec, st_spec],
        out_shape=[jax.ShapeDtypeStruct(q.shape, _F32), jax.ShapeDtypeStruct(s0.shape, _F32)],
        compiler_params=pltpu.CompilerParams(
            dimension_semantics=("arbitrary",), vmem_limit_bytes=VMEM_LIMIT),
        name="gla_sample",
    )(q, k, lf, v, s0)


def _row(v):
    return v.reshape(1, -1)


def _trunk_prompt(x, wts):
    nb, n, d = x.shape
    pool_ctx = jnp.zeros((nb, POOL_CTX, d), _F32)
    ffn_ctx = jnp.zeros((nb, CONV_W - 1, 2 * D_FF), _F32)
    s0 = jnp.zeros((nb, N_HEADS, HEAD_K, HEAD_V), _F32)

    x, pool_st = _pool_layer(x, pool_ctx, wts["mix_pre"][0], wts["pool_w"], wts["pool_scale"],
                             wts["mix_post"][0], rows=ROW_TILE, step=1, pos0=0)
    x, ffn_st0 = _ffn_layer(x, ffn_ctx, wts["ffn_pre"][0], wts["w_up"][0], wts["conv_w"][0],
                            wts["conv_b"][0], wts["w_down"][0], wts["ffn_post"][0],
                            rows=ROW_TILE, step=1)
    x2d = x.reshape(nb * n, d)
    q, k, lf, v, gs = _hgrn_proj(x2d, wts["mix_pre"][1], wts["w_in"], wts["lb_logits"],
                                 rows=ROW_TILE)
    shp = (nb, n, d)
    o, s_new = _gla_prompt(q.reshape(shp), k.reshape(shp), lf.reshape(shp), v.reshape(shp), s0,
                           rows=GLA_TILE)
    x2d = _hgrn_out(x2d, o.reshape(nb * n, d), gs, wts["gnorm"], wts["w_out"],
                    wts["mix_post"][1], rows=ROW_TILE)
    x, ffn_st1 = _ffn_layer(x2d.reshape(shp), ffn_ctx, wts["ffn_pre"][1], wts["w_up"][1],
                            wts["conv_w"][1], wts["conv_b"][1], wts["w_down"][1],
                            wts["ffn_post"][1], rows=ROW_TILE, step=1)
    return x, pool_st[None], s_new[None], jnp.stack([ffn_st0, ffn_st1])


def _trunk_sample(x, pos0, state_pool, state_hgrn, state_ffn, wts):
    nb, steps, d = x.shape
    n = nb * steps

    def to_time_major(a):
        return a.transpose(1, 0, 2).reshape(1, a.shape[1] * nb, a.shape[2])

    def to_batch_major(a, t):
        return a.reshape(t, nb, a.shape[-1]).transpose(1, 0, 2)

    xt = to_time_major(x)
    xt, pool_st = _pool_layer(xt, to_time_major(state_pool[0]), wts["mix_pre"][0], wts["pool_w"],
                              wts["pool_scale"], wts["mix_post"][0], rows=n, step=nb, pos0=pos0)
    xt, ffn_st0 = _ffn_layer(xt, to_time_major(state_ffn[0]), wts["ffn_pre"][0], wts["w_up"][0],
                             wts["conv_w"][0], wts["conv_b"][0], wts["w_down"][0],
                             wts["ffn_post"][0], rows=n, step=nb)
    x2d = xt.reshape(n, d)
    proj = _hgrn_proj(x2d, wts["mix_pre"][1], wts["w_in"], wts["lb_logits"], rows=n)

    def pad_time(a):
        a = to_batch_major(a, steps)
        return jnp.pad(a, ((0, 0), (0, SAMPLE_T_PAD - steps), (0, 0)))

    q, k, lf, v = (pad_time(a) for a in proj[:4])
    o, s_new = _gla_sample(q, k, lf, v, state_hgrn[0], steps=steps)
    o2d = o[:, :steps].transpose(1, 0, 2).reshape(n, d)
    x2d = _hgrn_out(x2d, o2d, proj[4], wts["gnorm"], wts["w_out"], wts["mix_post"][1], rows=n)
    xt, ffn_st1 = _ffn_layer(x2d.reshape(1, n, d), to_time_major(state_ffn[1]), wts["ffn_pre"][1],
                             wts["w_up"][1], wts["conv_w"][1], wts["conv_b"][1],
                             wts["w_down"][1], wts["ffn_post"][1], rows=n, step=nb)
    y = to_batch_major(xt, steps)
    new_pool = to_batch_major(pool_st, POOL_CTX)[None]
    new_ffn = jnp.stack([to_batch_major(ffn_st0, CONV_W - 1), to_batch_major(ffn_st1, CONV_W - 1)])
    return y, new_pool, s_new[None], new_ffn


def kernel(x_prompt, x_sample, state_pool, state_hgrn, state_ffn_conv, norm_mix_pre, norm_mix_post,
           norm_ffn_pre, norm_ffn_post, pool_w, pool_scale, hgrn_w_in, hgrn_lb_logits, hgrn_gnorm,
           hgrn_w_out, ffn_w_up, ffn_conv_w, ffn_conv_b, ffn_w_down):
    depth = ffn_w_up.shape[0]
    assert depth == 2 and pool_w.shape[0] == 1 and hgrn_w_in.shape[0] == 1
    past_len = 16384
    wts = {
        "mix_pre": [_row(norm_mix_pre[i]) for i in range(depth)],
        "mix_post": [_row(norm_mix_post[i]) for i in range(depth)],
        "ffn_pre": [_row(norm_ffn_pre[i]) for i in range(depth)],
        "ffn_post": [_row(norm_ffn_post[i]) for i in range(depth)],
        "pool_w": pool_w[0].astype(_BF16),
        "pool_scale": _row(pool_scale[0]),
        "w_in": hgrn_w_in[0].astype(_BF16),
        "lb_logits": hgrn_lb_logits,
        "gnorm": _row(hgrn_gnorm[0]),
        "w_out": hgrn_w_out[0].astype(_BF16),
        "w_up": [ffn_w_up[i].astype(_BF16) for i in range(depth)],
        "conv_w": [ffn_conv_w[i] for i in range(depth)],
        "conv_b": [_row(ffn_conv_b[i]) for i in range(depth)],
        "w_down": [ffn_w_down[i].astype(_BF16) for i in range(depth)],
    }
    y_p, pool_p, hgrn_p, ffn_p = _trunk_prompt(x_prompt, wts)
    y_s, pool_s, hgrn_s, ffn_s = _trunk_sample(x_sample, past_len, state_pool, state_hgrn,
                                               state_ffn_conv, wts)
    return (y_p, y_s, pool_p, pool_s, hgrn_p, hgrn_s, ffn_p, ffn_s)
```

```python
import functools

import jax
import jax.numpy as jnp
import numpy as np
from jax import lax
from jax.experimental import pallas as pl
from jax.experimental.pallas import tpu as pltpu

D_MODEL = 1024
POOL_WINDOWS = (2, 4, 8, 16)
POOL_GROUP_DIM = D_MODEL // len(POOL_WINDOWS)
POOL_CTX = max(POOL_WINDOWS) - 1
N_HEADS = 8
HEAD_K = 128
HEAD_V = D_MODEL // N_HEADS
F_DIM = N_HEADS * HEAD_K
V_DIM = N_HEADS * HEAD_V
D_FF = 2816
CONV_W = 3
EPS = 1e-6

SUBLANES = 8
ROW_TILE = 512
FF_CHUNK = 256
GLA_CHUNK = 64
GLA_LEVELS = 6
GLA_TILE = 256
SAMPLE_T_PAD = 8
SAMPLE_B_BLK = 8
VMEM_LIMIT = 56 * 1024 * 1024

_F32 = jnp.float32
_BF16 = jnp.bfloat16


def _rms(x, g):
    ms = jnp.mean(x * x, axis=-1, keepdims=True)
    return x * lax.rsqrt(ms + EPS) * g


def _dot(a, b):
    return jnp.dot(a, b, preferred_element_type=_F32)


def _dot_nt(a, b):
    return lax.dot_general(a, b, (((1,), (1,)), ((), ())), preferred_element_type=_F32)


def _dot_tn(a, b):
    return lax.dot_general(a, b, (((0,), (0,)), ((), ())), preferred_element_type=_F32)


def _const_spec(shape):
    nd = len(shape)
    return pl.BlockSpec(shape, lambda *_: (0,) * nd, pipeline_mode=pl.Buffered(1))


def _params():
    return pltpu.CompilerParams(
        dimension_semantics=("arbitrary", "arbitrary"), vmem_limit_bytes=VMEM_LIMIT)


def _pool_kernel(x_ref, ctx_ref, gpre_ref, w_ref, scale_ref, gpost_ref, y_ref, st_ref, hbuf,
                 *, rows, step, pos0, n_tiles):
    halo = (POOL_CTX + 1) * step
    l = pl.program_id(1)

    @pl.when(l == 0)
    def _():
        hbuf[0:step, :] = jnp.zeros((step, D_MODEL), _F32)
        hbuf[step:halo, :] = ctx_ref[0]

    x = x_ref[0]
    h = _rms(x, gpre_ref[...])
    hbuf[halo:halo + rows, :] = h

    if pos0 < POOL_CTX:
        assert step == 1
        pos = pos0 + l * rows + lax.broadcasted_iota(jnp.int32, (rows, 1), 0)
    outs = []
    for g, w in enumerate(POOL_WINDOWS):
        cs = slice(g * POOL_GROUP_DIM, (g + 1) * POOL_GROUP_DIM)
        s = h[:, cs]
        for j in range(1, w):
            s = s + hbuf[halo - j * step:halo - j * step + rows, cs]
        if pos0 < POOL_CTX:
            cnt = jnp.minimum(pos + 1, w).astype(_F32)
        else:
            cnt = float(w)
        p = s / cnt - h[:, cs]
        outs.append(_dot(p.astype(_BF16), w_ref[g]))
    m = jnp.concatenate(outs, axis=-1) * scale_ref[...]
    y_ref[0] = x + _rms(m, gpost_ref[...])

    @pl.when(l == n_tiles - 1)
    def _():
        st_ref[0] = hbuf[rows + step:rows + halo, :]

    if n_tiles > 1:
        hbuf[0:halo, :] = hbuf[rows:rows + halo, :]


def _pool_layer(x, ctx, gpre, w_bf16, scale, gpost, *, rows, step, pos0):
    nb, n, _ = x.shape
    n_tiles = n // rows
    halo = (POOL_CTX + 1) * step
    ctx_rows = POOL_CTX * step
    kern = functools.partial(_pool_kernel, rows=rows, step=step, pos0=pos0, n_tiles=n_tiles)
    return pl.pallas_call(
        kern,
        grid=(nb, n_tiles),
        in_specs=[
            pl.BlockSpec((1, rows, D_MODEL), lambda b, l: (b, l, 0)),
            pl.BlockSpec((1, ctx_rows, D_MODEL), lambda b, l: (b, 0, 0)),
            _const_spec((1, D_MODEL)),
            _const_spec(w_bf16.shape),
            _const_spec((1, D_MODEL)),
            _const_spec((1, D_MODEL)),
        ],
        out_specs=[
            pl.BlockSpec((1, rows, D_MODEL), lambda b, l: (b, l, 0)),
            pl.BlockSpec((1, ctx_rows, D_MODEL), lambda b, l: (b, 0, 0)),
        ],
        out_shape=[
            jax.ShapeDtypeStruct(x.shape, _F32),
            jax.ShapeDtypeStruct((nb, ctx_rows, D_MODEL), _F32),
        ],
        scratch_shapes=[pltpu.VMEM((halo + rows, D_MODEL), _F32)],
        compiler_params=_params(),
        name="pool_mixer",
    )(x, ctx, gpre, w_bf16, scale, gpost)


def _ffn_kernel(x_ref, ctx_ref, gpre_ref, wup_ref, cw_ref, cb_ref, wdn_ref, gpost_ref,
                y_ref, st_ref, carry, gbuf, vbuf, abuf, *, rows, step, n_tiles):
    halo = max(SUBLANES, (CONV_W - 1) * step)
    l = pl.program_id(1)

    @pl.when(l == 0)
    def _():
        if halo > (CONV_W - 1) * step:
            carry[0:halo - (CONV_W - 1) * step, :] = jnp.zeros(
                (halo - (CONV_W - 1) * step, 2 * D_FF), _F32)
        carry[halo - (CONV_W - 1) * step:halo, :] = ctx_ref[0]

    x = x_ref[0]
    h = _rms(x, gpre_ref[...]).astype(_BF16)
    for c in range(D_FF // FF_CHUNK):
        conv = []
        for buf, off in ((gbuf, 0), (vbuf, D_FF)):
            cs = slice(off + c * FF_CHUNK, off + (c + 1) * FF_CHUNK)
            u = _dot(h, wup_ref[:, cs])
            buf[0:halo, :] = carry[:, cs]
            buf[halo:halo + rows, :] = u
            carry[:, cs] = buf[rows:rows + halo, :]
            conv.append(cb_ref[:, cs]
                        + buf[halo - 2 * step:halo - 2 * step + rows, :] * cw_ref[0:1, cs]
                        + buf[halo - step:halo - step + rows, :] * cw_ref[1:2, cs]
                        + u * cw_ref[2:3, cs])
        a = jax.nn.gelu(conv[0], approximate=True) * conv[1]
        abuf[:, c * FF_CHUNK:(c + 1) * FF_CHUNK] = a.astype(_BF16)
    m = _dot(abuf[...], wdn_ref[...])
    y_ref[0] = x + _rms(m, gpost_ref[...])

    @pl.when(l == n_tiles - 1)
    def _():
        st_ref[0] = carry[halo - (CONV_W - 1) * step:halo, :]


def _ffn_layer(x, ctx, gpre, wup_bf16, cw, cb, wdn_bf16, gpost, *, rows, step):
    nb, n, _ = x.shape
    n_tiles = n // rows
    halo = max(SUBLANES, (CONV_W - 1) * step)
    ctx_rows = (CONV_W - 1) * step
    kern = functools.partial(_ffn_kernel, rows=rows, step=step, n_tiles=n_tiles)
    return pl.pallas_call(
        kern,
        grid=(nb, n_tiles),
        in_specs=[
            pl.BlockSpec((1, rows, D_MODEL), lambda b, l: (b, l, 0)),
            pl.BlockSpec((1, ctx_rows, 2 * D_FF), lambda b, l: (b, 0, 0)),
            _const_spec((1, D_MODEL)),
            _const_spec((D_MODEL, 2 * D_FF)),
            _const_spec((CONV_W, 2 * D_FF)),
            _const_spec((1, 2 * D_FF)),
            _const_spec((D_FF, D_MODEL)),
            _const_spec((1, D_MODEL)),
        ],
        out_specs=[
            pl.BlockSpec((1, rows, D_MODEL), lambda b, l: (b, l, 0)),
            pl.BlockSpec((1, ctx_rows, 2 * D_FF), lambda b, l: (b, 0, 0)),
        ],
        out_shape=[
            jax.ShapeDtypeStruct(x.shape, _F32),
            jax.ShapeDtypeStruct((nb, ctx_rows, 2 * D_FF), _F32),
        ],
        scratch_shapes=[
            pltpu.VMEM((halo, 2 * D_FF), _F32),
            pltpu.VMEM((halo + rows, FF_CHUNK), _F32),
            pltpu.VMEM((halo + rows, FF_CHUNK), _F32),
            pltpu.VMEM((rows, D_FF), _BF16),
        ],
        compiler_params=_params(),
        name="conv_ffn",
    )(x, ctx, gpre, wup_bf16, cw, cb, wdn_bf16, gpost)


def _hproj_kernel(x_ref, gpre_ref, win_ref, lbl_ref, q_ref, k_ref, lf_ref, v_ref, gs_ref):
    h = _rms(x_ref[...], gpre_ref[...]).astype(_BF16)
    l0 = lbl_ref[0:1, :]
    l1 = lbl_ref[1:2, :]
    mx = jnp.maximum(l0, l1)
    e0 = jnp.exp(l0 - mx)
    e1 = jnp.exp(l1 - mx)
    lb = e1 / (e0 + e1)

    qr = _dot(h, win_ref[:, 0:F_DIM])
    q_ref[...] = qr * jax.nn.sigmoid(qr) * (HEAD_K ** -0.5)
    fr = _dot(h, win_ref[:, F_DIM:2 * F_DIM])
    lf_ref[...] = jnp.log(lb + (1.0 - lb) * jax.nn.sigmoid(fr))
    k_ref[...] = (1.0 - lb) * jax.nn.sigmoid(-fr)
    v_ref[...] = _dot(h, win_ref[:, 2 * F_DIM:2 * F_DIM + V_DIM])
    gr = _dot(h, win_ref[:, 2 * F_DIM + V_DIM:2 * F_DIM + 2 * V_DIM])
    gs_ref[...] = gr * jax.nn.sigmoid(gr)


def _hgrn_proj(x2d, gpre, win_bf16, lb_logits, *, rows):
    n = x2d.shape[0]
    row_spec = pl.BlockSpec((rows, D_MODEL), lambda i: (i, 0))
    out = jax.ShapeDtypeStruct((n, D_MODEL), _F32)
    return pl.pallas_call(
        _hproj_kernel,
        grid=(n // rows,),
        in_specs=[row_spec, _const_spec((1, D_MODEL)), _const_spec(win_bf16.shape),
                  _const_spec(lb_logits.shape)],
        out_specs=[row_spec] * 5,
        out_shape=[out] * 5,
        compiler_params=pltpu.CompilerParams(
            dimension_semantics=("arbitrary",), vmem_limit_bytes=VMEM_LIMIT),
        name="hgrn_proj",
    )(x2d, gpre, win_bf16, lb_logits)


def _hout_kernel(x_ref, o_ref, gs_ref, gn_ref, wout_ref, gpost_ref, y_ref, zbuf):
    for hd in range(N_HEADS):
        hs = slice(hd * HEAD_V, (hd + 1) * HEAD_V)
        zbuf[:, hs] = (_rms(o_ref[:, hs], gn_ref[...]) * gs_ref[:, hs]).astype(_BF16)
    m = _dot(zbuf[...], wout_ref[...])
    y_ref[...] = x_ref[...] + _rms(m, gpost_ref[...])


def _hgrn_out(x2d, o2d, gs2d, gnorm, wout_bf16, gpost, *, rows):
    n = x2d.shape[0]
    row_spec = pl.BlockSpec((rows, D_MODEL), lambda i: (i, 0))
    return pl.pallas_call(
        _hout_kernel,
        grid=(n // rows,),
        in_specs=[row_spec, row_spec, row_spec, _const_spec((1, HEAD_V)),
                  _const_spec(wout_bf16.shape), _const_spec((1, D_MODEL))],
        out_specs=row_spec,
        out_shape=jax.ShapeDtypeStruct((n, D_MODEL), _F32),
        scratch_shapes=[pltpu.VMEM((rows, V_DIM), _BF16)],
        compiler_params=pltpu.CompilerParams(
            dimension_semantics=("arbitrary",), vmem_limit_bytes=VMEM_LIMIT),
        name="hgrn_out",
    )(x2d, o2d, gs2d, gnorm, wout_bf16, gpost)


def _gla_exponent_matrix():
    c = GLA_CHUNK
    t = np.arange(c)[:, None]
    w = np.arange(c)[None, :]
    blocks = [(w <= t), (w > t)]
    for lev in range(1, GLA_LEVELS + 1):
        n = 1 << lev
        mid = (t // n) * n + n // 2 - 1
        later = (t > mid) & (w > mid) & (w <= t)
        earlier = (t <= mid) & (w > t) & (w <= mid)
        blocks.append(later | earlier)
    return np.concatenate(blocks, axis=0).astype(np.float32)


def _gla_kernel(q_ref, k_ref, lf_ref, v_ref, s0_ref, mall_ref, o_ref, sn_ref, st,
                *, rows, n_tiles):
    c = GLA_CHUNK
    l = pl.program_id(1)

    @pl.when(l == 0)
    def _():
        for hd in range(N_HEADS):
            st[hd] = s0_ref[0, hd].T

    ti = lax.broadcasted_iota(jnp.int32, (c, c), 0)
    si = lax.broadcasted_iota(jnp.int32, (c, c), 1)
    ri = lax.broadcasted_iota(jnp.int32, (c, 1), 0)
    diag = ti == si
    lvl_mask, lvl_role = [], []
    for lev in range(1, GLA_LEVELS + 1):
        same = (ti >> lev) == (si >> lev)
        t_late = ((ti >> (lev - 1)) & 1) == 1
        s_early = ((si >> (lev - 1)) & 1) == 0
        lvl_mask.append(same & t_late & s_early)
        lvl_role.append(((ri >> (lev - 1)) & 1) == 1)

    def chunk(ci, carry):
        rs = pl.ds(pl.multiple_of(ci * c, c), c)
        lf = lf_ref[0, rs, :]
        lf_hi = lf.astype(_BF16)
        r1 = lf - lf_hi.astype(_F32)
        lf_mid = r1.astype(_BF16)
        lf_lo = (r1 - lf_mid.astype(_F32)).astype(_BF16)
        ex = jnp.exp(_dot(mall_ref[...], jnp.concatenate([lf_hi, lf_mid, lf_lo], axis=0)))
        eb = ex[0:c]
        eb_rest = ex[c:2 * c]
        eb_last = eb[c - 1:c]
        q = q_ref[0, rs, :]
        k = k_ref[0, rs, :]
        v = v_ref[0, rs, :]
        qd = (q * eb).astype(_BF16)
        kd = (k * eb_rest).astype(_BF16)
        heads = [slice(hd * HEAD_K, (hd + 1) * HEAD_K) for hd in range(N_HEADS)]
        vb = v.astype(_BF16)
        scores = []
        for hs in heads:
            qh = q[:, hs]
            kh = k[:, hs]
            a = jnp.where(diag, _dot_nt(qh.astype(_BF16), kh.astype(_BF16)), 0.0)
            for lev in range(GLA_LEVELS):
                e = ex[(2 + lev) * c:(3 + lev) * c, hs]
                xl = (jnp.where(lvl_role[lev], qh, kh) * e).astype(_BF16)
                a = jnp.where(lvl_mask[lev], _dot_nt(xl, xl), a)
            scores.append(a.astype(_BF16))
        for hd, hs in enumerate(heads):
            o_ref[0, rs, hs] = (_dot_nt(qd[:, hs], st[hd].astype(_BF16))
                                + _dot(scores[hd], vb[:, hs]))
        for hd, hs in enumerate(heads):
            st[hd] = st[hd] * eb_last[:, hs] + _dot_tn(vb[:, hs], kd[:, hs])
        return carry

    lax.fori_loop(0, rows // c, chunk, 0)

    @pl.when(l == n_tiles - 1)
    def _():
        for hd in range(N_HEADS):
            sn_ref[0, hd] = st[hd].T


def _gla_prompt(q, k, lf, v, s0, *, rows):
    nb, n, _ = q.shape
    n_tiles = n // rows
    mall = jnp.asarray(np.tile(_gla_exponent_matrix(), (1, 3)), dtype=_BF16)
    row_spec = pl.BlockSpec((1, rows, D_MODEL), lambda b, l: (b, l, 0))
    st_spec = pl.BlockSpec((1, N_HEADS, HEAD_K, HEAD_V), lambda b, l: (b, 0, 0, 0))
    kern = functools.partial(_gla_kernel, rows=rows, n_tiles=n_tiles)
    return pl.pallas_call(
        kern,
        grid=(nb, n_tiles),
        in_specs=[row_spec, row_spec, row_spec, row_spec, st_spec, _const_spec(mall.shape)],
        out_specs=[row_spec, st_spec],
        out_shape=[jax.ShapeDtypeStruct(q.shape, _F32), jax.ShapeDtypeStruct(s0.shape, _F32)],
        scratch_shapes=[pltpu.VMEM((N_HEADS, HEAD_V, HEAD_K), _F32)],
        compiler_params=_params(),
        name="gla_prompt",
    )(q, k, lf, v, s0, mall)


def _gla_sample_kernel(q_ref, k_ref, lf_ref, v_ref, s0_ref, o_ref, sn_ref, *, steps):
    tp = SAMPLE_T_PAD
    ri = lax.broadcasted_iota(jnp.int32, (tp, 1), 0)
    ri128 = lax.broadcasted_iota(jnp.int32, (HEAD_K, HEAD_K), 0)
    ci128 = lax.broadcasted_iota(jnp.int32, (HEAD_K, HEAD_K), 1)
    eye = (ri128 == ci128).astype(_F32)
    for bi in range(SAMPLE_B_BLK):
        lf = lf_ref[bi]
        q = q_ref[bi]
        k = k_ref[bi]
        v = v_ref[bi]
        b = jnp.zeros_like(lf)
        for j in range(steps):
            b = b + jnp.where(ri >= j, lf[j:j + 1, :], 0.0)
        eb = jnp.exp(b)
        b_last = b[steps - 1:steps, :]
        eb_last = jnp.exp(b_last)
        qd = (q * eb).astype(_BF16)
        kd = k * jnp.exp(b_last - b)
        o_intra = jnp.zeros_like(q)
        for s in range(steps):
            arg = jnp.where(ri >= s, b - b[s:s + 1, :], 0.0)
            p = jnp.where(ri >= s, q * k[s:s + 1, :] * jnp.exp(arg), 0.0)
            parts = []
            for hd in range(N_HEADS):
                hs = slice(hd * HEAD_K, (hd + 1) * HEAD_K)
                a_s = jnp.sum(p[:, hs], axis=-1, keepdims=True)
                parts.append(a_s * v[s:s + 1, hs])
            o_intra = o_intra + jnp.concatenate(parts, axis=-1)
        for hd in range(N_HEADS):
            hs = slice(hd * HEAD_K, (hd + 1) * HEAD_K)
            s0 = s0_ref[bi, hd]
            o_ref[bi, :, hs] = o_intra[:, hs] + _dot(qd[:, hs], s0.astype(_BF16))
            dcol = jnp.sum(eye * eb_last[:, hs], axis=-1, keepdims=True)
            upd = dcol * s0
            for s in range(steps):
                kcol = jnp.sum(eye * kd[s:s + 1, hs], axis=-1, keepdims=True)
                upd = upd + kcol * v[s:s + 1, hs]
            sn_ref[bi, hd] = upd


def _gla_sample(q, k, lf, v, s0, *, steps):
    nb = q.shape[0]
    row_spec = pl.BlockSpec((SAMPLE_B_BLK, SAMPLE_T_PAD, D_MODEL), lambda i: (i, 0, 0))
    st_spec = pl.BlockSpec((SAMPLE_B_BLK, N_HEADS, HEAD_K, HEAD_V), lambda i: (i, 0, 0, 0))
    kern = functools.partial(_gla_sample_kernel, steps=steps)
    return pl.pallas_call(
        kern,
        grid=(nb // SAMPLE_B_BLK,),
        in_specs=[row_spec, row_spec, row_spec, row_spec, st_spec],
        out_specs=[row_spec, st_spec],
        out_shape=[jax.ShapeDtypeStruct(q.shape, _F32), jax.ShapeDtypeStruct(s0.shape, _F32)],
        compiler_params=pltpu.CompilerParams(
            dimension_semantics=("arbitrary",), vmem_limit_bytes=VMEM_LIMIT),
        name="gla_sample",
    )(q, k, lf, v, s0)


def _row(v):
    return v.reshape(1, -1)


def _trunk_prompt(x, wts):
    nb, n, d = x.shape
    pool_ctx = jnp.zeros((nb, POOL_CTX, d), _F32)
    ffn_ctx = jnp.zeros((nb, CONV_W - 1, 2 * D_FF), _F32)
    s0 = jnp.zeros((nb, N_HEADS, HEAD_K, HEAD_V), _F32)

    x, pool_st = _pool_layer(x, pool_ctx, wts["mix_pre"][0], wts["pool_w"], wts["pool_scale"],
                             wts["mix_post"][0], rows=ROW_TILE, step=1, pos0=0)
    x, ffn_st0 = _ffn_layer(x, ffn_ctx, wts["ffn_pre"][0], wts["w_up"][0], wts["conv_w"][0],
                            wts["conv_b"][0], wts["w_down"][0], wts["ffn_post"][0],
                            rows=ROW_TILE, step=1)
    x2d = x.reshape(nb * n, d)
    q, k, lf, v, gs = _hgrn_proj(x2d, wts["mix_pre"][1], wts["w_in"], wts["lb_logits"],
                                 rows=ROW_TILE)
    shp = (nb, n, d)
    o, s_new = _gla_prompt(q.reshape(shp), k.reshape(shp), lf.reshape(shp), v.reshape(shp), s0,
                           rows=GLA_TILE)
    x2d = _hgrn_out(x2d, o.reshape(nb * n, d), gs, wts["gnorm"], wts["w_out"],
                    wts["mix_post"][1], rows=ROW_TILE)
    x, ffn_st1 = _ffn_layer(x2d.reshape(shp), ffn_ctx, wts["ffn_pre"][1], wts["w_up"][1],
                            wts["conv_w"][1], wts["conv_b"][1], wts["w_down"][1],
                            wts["ffn_post"][1], rows=ROW_TILE, step=1)
    return x, pool_st[None], s_new[None], jnp.stack([ffn_st0, ffn_st1])


def _trunk_sample(x, pos0, state_pool, state_hgrn, state_ffn, wts):
    nb, steps, d = x.shape
    n = nb * steps

    def to_time_major(a):
        return a.transpose(1, 0, 2).reshape(1, a.shape[1] * nb, a.shape[2])

    def to_batch_major(a, t):
        return a.reshape(t, nb, a.shape[-1]).transpose(1, 0, 2)

    xt = to_time_major(x)
    xt, pool_st = _pool_layer(xt, to_time_major(state_pool[0]), wts["mix_pre"][0], wts["pool_w"],
                              wts["pool_scale"], wts["mix_post"][0], rows=n, step=nb, pos0=pos0)
    xt, ffn_st0 = _ffn_layer(xt, to_time_major(state_ffn[0]), wts["ffn_pre"][0], wts["w_up"][0],
                             wts["conv_w"][0], wts["conv_b"][0], wts["w_down"][0],
                             wts["ffn_post"][0], rows=n, step=nb)
    x2d = xt.reshape(n, d)
    proj = _hgrn_proj(x2d, wts["mix_pre"][1], wts["w_in"], wts["lb_logits"], rows=n)

    def pad_time(a):
        a = to_batch_major(a, steps)
        return jnp.pad(a, ((0, 0), (0, SAMPLE_T_PAD - steps), (0, 0)))

    q, k, lf, v = (pad_time(a) for a in proj[:4])
    o, s_new = _gla_sample(q, k, lf, v, state_hgrn[0], steps=steps)
    o2d = o[:, :steps].transpose(1, 0, 2).reshape(n, d)
    x2d = _hgrn_out(x2d, o2d, proj[4], wts["gnorm"], wts["w_out"], wts["mix_post"][1], rows=n)
    xt, ffn_st1 = _ffn_layer(x2d.reshape(1, n, d), to_time_major(state_ffn[1]), wts["ffn_pre"][1],
                             wts["w_up"][1], wts["conv_w"][1], wts["conv_b"][1],
                             wts["w_down"][1], wts["ffn_post"][1], rows=n, step=nb)
    y = to_batch_major(xt, steps)
    new_pool = to_batch_major(pool_st, POOL_CTX)[None]
    new_ffn = jnp.stack([to_batch_major(ffn_st0, CONV_W - 1), to_batch_major(ffn_st1, CONV_W - 1)])
    return y, new_pool, s_new[None], new_ffn


def kernel(x_prompt, x_sample, state_pool, state_hgrn, state_ffn_conv, norm_mix_pre, norm_mix_post,
           norm_ffn_pre, norm_ffn_post, pool_w, pool_scale, hgrn_w_in, hgrn_lb_logits, hgrn_gnorm,
           hgrn_w_out, ffn_w_up, ffn_conv_w, ffn_conv_b, ffn_w_down):
    depth = ffn_w_up.shape[0]
    assert depth == 2 and pool_w.shape[0] == 1 and hgrn_w_in.shape[0] == 1
    past_len = 16384
    wts = {
        "mix_pre": [_row(norm_mix_pre[i]) for i in range(depth)],
        "mix_post": [_row(norm_mix_post[i]) for i in range(depth)],
        "ffn_pre": [_row(norm_ffn_pre[i]) for i in range(depth)],
        "ffn_post": [_row(norm_ffn_post[i]) for i in range(depth)],
        "pool_w": pool_w[0].astype(_BF16),
        "pool_scale": _row(pool_scale[0]),
        "w_in": hgrn_w_in[0].astype(_BF16),
        "lb_logits": hgrn_lb_logits,
        "gnorm": _row(hgrn_gnorm[0]),
        "w_out": hgrn_w_out[0].astype(_BF16),
        "w_up": [ffn_w_up[i].astype(_BF16) for i in range(depth)],
        "conv_w": [ffn_conv_w[i] for i in range(depth)],
        "conv_b": [_row(ffn_conv_b[i]) for i in range(depth)],
        "w_down": [ffn_w_down[i].astype(_BF16) for i in range(depth)],
    }
    y_p, pool_p, hgrn_p, ffn_p = _trunk_prompt(x_prompt, wts)
    y_s, pool_s, hgrn_s, ffn_s = _trunk_sample(x_sample, past_len, state_pool, state_hgrn,
                                               state_ffn_conv, wts)
    return (y_p, y_s, pool_p, pool_s, hgrn_p, hgrn_s, ffn_p, ffn_s)
```

```python
import functools

import jax
import jax.numpy as jnp
import numpy as np
from jax import lax
from jax.experimental import pallas as pl
from jax.experimental.pallas import tpu as pltpu

D_MODEL = 1024
POOL_WINDOWS = (2, 4, 8, 16)
POOL_GROUP_DIM = D_MODEL // len(POOL_WINDOWS)
POOL_CTX = max(POOL_WINDOWS) - 1
N_HEADS = 8
HEAD_K = 128
HEAD_V = D_MODEL // N_HEADS
F_DIM = N_HEADS * HEAD_K
V_DIM = N_HEADS * HEAD_V
D_FF = 2816
CONV_W = 3
EPS = 1e-6

SUBLANES = 8
ROW_TILE = 512
FF_CHUNK = 256
GLA_CHUNK = 64
GLA_LEVELS = 6
GLA_TILE = 256
SAMPLE_T_PAD = 8
SAMPLE_B_BLK = 8
VMEM_LIMIT = 56 * 1024 * 1024

_F32 = jnp.float32
_BF16 = jnp.bfloat16


def _rms(x, g):
    ms = jnp.mean(x * x, axis=-1, keepdims=True)
    return x * lax.rsqrt(ms + EPS) * g


def _dot(a, b):
    return jnp.dot(a, b, preferred_element_type=_F32)


def _dot_nt(a, b):
    return lax.dot_general(a, b, (((1,), (1,)), ((), ())), preferred_element_type=_F32)


def _dot_tn(a, b):
    return lax.dot_general(a, b, (((0,), (0,)), ((), ())), preferred_element_type=_F32)


def _const_spec(shape):
    nd = len(shape)
    return pl.BlockSpec(shape, lambda *_: (0,) * nd, pipeline_mode=pl.Buffered(1))


def _params():
    return pltpu.CompilerParams(
        dimension_semantics=("arbitrary", "arbitrary"), vmem_limit_bytes=VMEM_LIMIT)


def _pool_kernel(x_ref, ctx_ref, gpre_ref, w_ref, scale_ref, gpost_ref, y_ref, st_ref, hbuf,
                 *, rows, step, pos0, n_tiles):
    halo = (POOL_CTX + 1) * step
    l = pl.program_id(1)

    @pl.when(l == 0)
    def _():
        hbuf[0:step, :] = jnp.zeros((step, D_MODEL), _F32)
        hbuf[step:halo, :] = ctx_ref[0]

    x = x_ref[0]
    h = _rms(x, gpre_ref[...])
    hbuf[halo:halo + rows, :] = h

    if pos0 < POOL_CTX:
        assert step == 1
        pos = pos0 + l * rows + lax.broadcasted_iota(jnp.int32, (rows, 1), 0)
    outs = []
    for g, w in enumerate(POOL_WINDOWS):
        cs = slice(g * POOL_GROUP_DIM, (g + 1) * POOL_GROUP_DIM)
        s = h[:, cs]
        for j in range(1, w):
            s = s + hbuf[halo - j * step:halo - j * step + rows, cs]
        if pos0 < POOL_CTX:
            cnt = jnp.minimum(pos + 1, w).astype(_F32)
        else:
            cnt = float(w)
        p = s / cnt - h[:, cs]
        outs.append(_dot(p.astype(_BF16), w_ref[g]))
    m = jnp.concatenate(outs, axis=-1) * scale_ref[...]
    y_ref[...] = (x + _rms(m, gpost_ref[...])).reshape(y_ref.shape)

    @pl.when(l == n_tiles - 1)
    def _():
        st_ref[0] = hbuf[rows + step:rows + halo, :]

    if n_tiles > 1:
        hbuf[0:halo, :] = hbuf[rows:rows + halo, :]


def _pool_layer(x, ctx, gpre, w_bf16, scale, gpost, *, rows, step, pos0, time_major_out=False):
    nb, n, _ = x.shape
    n_tiles = n // rows
    halo = (POOL_CTX + 1) * step
    ctx_rows = POOL_CTX * step
    kern = functools.partial(_pool_kernel, rows=rows, step=step, pos0=pos0, n_tiles=n_tiles)
    if time_major_out:
        y_spec = pl.BlockSpec((rows, D_MODEL), lambda b, l: (l, b))
        y_shape = jax.ShapeDtypeStruct((n, nb * D_MODEL), _F32)
    else:
        y_spec = pl.BlockSpec((1, rows, D_MODEL), lambda b, l: (b, l, 0))
        y_shape = jax.ShapeDtypeStruct(x.shape, _F32)
    return pl.pallas_call(
        kern,
        grid=(nb, n_tiles),
        in_specs=[
            pl.BlockSpec((1, rows, D_MODEL), lambda b, l: (b, l, 0)),
            pl.BlockSpec((1, ctx_rows, D_MODEL), lambda b, l: (b, 0, 0)),
            _const_spec((1, D_MODEL)),
            _const_spec(w_bf16.shape),
            _const_spec((1, D_MODEL)),
            _const_spec((1, D_MODEL)),
        ],
        out_specs=[
            y_spec,
            pl.BlockSpec((1, ctx_rows, D_MODEL), lambda b, l: (b, 0, 0)),
        ],
        out_shape=[
            y_shape,
            jax.ShapeDtypeStruct((nb, ctx_rows, D_MODEL), _F32),
        ],
        scratch_shapes=[pltpu.VMEM((halo + rows, D_MODEL), _F32)],
        compiler_params=_params(),
        name="pool_mixer",
    )(x, ctx, gpre, w_bf16, scale, gpost)


def _ffn_kernel(x_ref, ctx_ref, gpre_ref, wup_ref, cw_ref, cb_ref, wdn_ref, gpost_ref,
                y_ref, st_ref, carry, gbuf, vbuf, abuf, *, rows, step, n_tiles):
    halo = max(SUBLANES, (CONV_W - 1) * step)
    l = pl.program_id(1)

    @pl.when(l == 0)
    def _():
        if halo > (CONV_W - 1) * step:
            carry[0:halo - (CONV_W - 1) * step, :] = jnp.zeros(
                (halo - (CONV_W - 1) * step, 2 * D_FF), _F32)
        carry[halo - (CONV_W - 1) * step:halo, :] = ctx_ref[0]

    x = x_ref[0]
    h = _rms(x, gpre_ref[...]).astype(_BF16)
    for c in range(D_FF // FF_CHUNK):
        conv = []
        for buf, off in ((gbuf, 0), (vbuf, D_FF)):
            cs = slice(off + c * FF_CHUNK, off + (c + 1) * FF_CHUNK)
            u = _dot(h, wup_ref[:, cs])
            buf[0:halo, :] = carry[:, cs]
            buf[halo:halo + rows, :] = u
            carry[:, cs] = buf[rows:rows + halo, :]
            conv.append(cb_ref[:, cs]
                        + buf[halo - 2 * step:halo - 2 * step + rows, :] * cw_ref[0:1, cs]
                        + buf[halo - step:halo - step + rows, :] * cw_ref[1:2, cs]
                        + u * cw_ref[2:3, cs])
        a = jax.nn.gelu(conv[0], approximate=True) * conv[1]
        abuf[:, c * FF_CHUNK:(c + 1) * FF_CHUNK] = a.astype(_BF16)
    m = _dot(abuf[...], wdn_ref[...])
    y_ref[0] = x + _rms(m, gpost_ref[...])

    @pl.when(l == n_tiles - 1)
    def _():
        st_ref[0] = carry[halo - (CONV_W - 1) * step:halo, :]


def _ffn_layer(x, ctx, gpre, wup_bf16, cw, cb, wdn_bf16, gpost, *, rows, step):
    nb, n, _ = x.shape
    n_tiles = n // rows
    halo = max(SUBLANES, (CONV_W - 1) * step)
    ctx_rows = (CONV_W - 1) * step
    kern = functools.partial(_ffn_kernel, rows=rows, step=step, n_tiles=n_tiles)
    return pl.pallas_call(
        kern,
        grid=(nb, n_tiles),
        in_specs=[
            pl.BlockSpec((1, rows, D_MODEL), lambda b, l: (b, l, 0)),
            pl.BlockSpec((1, ctx_rows, 2 * D_FF), lambda b, l: (b, 0, 0)),
            _const_spec((1, D_MODEL)),
            _const_spec((D_MODEL, 2 * D_FF)),
            _const_spec((CONV_W, 2 * D_FF)),
            _const_spec((1, 2 * D_FF)),
            _const_spec((D_FF, D_MODEL)),
            _const_spec((1, D_MODEL)),
        ],
        out_specs=[
            pl.BlockSpec((1, rows, D_MODEL), lambda b, l: (b, l, 0)),
            pl.BlockSpec((1, ctx_rows, 2 * D_FF), lambda b, l: (b, 0, 0)),
        ],
        out_shape=[
            jax.ShapeDtypeStruct(x.shape, _F32),
            jax.ShapeDtypeStruct((nb, ctx_rows, 2 * D_FF), _F32),
        ],
        scratch_shapes=[
            pltpu.VMEM((halo, 2 * D_FF), _F32),
            pltpu.VMEM((halo + rows, FF_CHUNK), _F32),
            pltpu.VMEM((halo + rows, FF_CHUNK), _F32),
            pltpu.VMEM((rows, D_FF), _BF16),
        ],
        compiler_params=_params(),
        name="conv_ffn",
    )(x, ctx, gpre, wup_bf16, cw, cb, wdn_bf16, gpost)


def _hproj_kernel(x_ref, gpre_ref, win_ref, lbl_ref, q_ref, k_ref, lf_ref, v_ref, gs_ref):
    h = _rms(x_ref[...], gpre_ref[...]).astype(_BF16)
    l0 = lbl_ref[0:1, :]
    l1 = lbl_ref[1:2, :]
    mx = jnp.maximum(l0, l1)
    e0 = jnp.exp(l0 - mx)
    e1 = jnp.exp(l1 - mx)
    lb = e1 / (e0 + e1)

    qr = _dot(h, win_ref[:, 0:F_DIM])
    q_ref[...] = qr * jax.nn.sigmoid(qr) * (HEAD_K ** -0.5)
    fr = _dot(h, win_ref[:, F_DIM:2 * F_DIM])
    lf_ref[...] = jnp.log(lb + (1.0 - lb) * jax.nn.sigmoid(fr))
    k_ref[...] = (1.0 - lb) * jax.nn.sigmoid(-fr)
    v_ref[...] = _dot(h, win_ref[:, 2 * F_DIM:2 * F_DIM + V_DIM])
    gr = _dot(h, win_ref[:, 2 * F_DIM + V_DIM:2 * F_DIM + 2 * V_DIM])
    gs_ref[...] = gr * jax.nn.sigmoid(gr)


def _hgrn_proj(x2d, gpre, win_bf16, lb_logits, *, rows):
    n = x2d.shape[0]
    row_spec = pl.BlockSpec((rows, D_MODEL), lambda i: (i, 0))
    out = jax.ShapeDtypeStruct((n, D_MODEL), _F32)
    return pl.pallas_call(
        _hproj_kernel,
        grid=(n // rows,),
        in_specs=[row_spec, _const_spec((1, D_MODEL)), _const_spec(win_bf16.shape),
                  _const_spec(lb_logits.shape)],
        out_specs=[row_spec] * 5,
        out_shape=[out] * 5,
        compiler_params=pltpu.CompilerParams(
            dimension_semantics=("arbitrary",), vmem_limit_bytes=VMEM_LIMIT),
        name="hgrn_proj",
    )(x2d, gpre, win_bf16, lb_logits)


def _hout_kernel(x_ref, o_ref, gs_ref, gn_ref, wout_ref, gpost_ref, y_ref, zbuf):
    for hd in range(N_HEADS):
        hs = slice(hd * HEAD_V, (hd + 1) * HEAD_V)
        zbuf[:, hs] = (_rms(o_ref[:, hs], gn_ref[...]) * gs_ref[:, hs]).astype(_BF16)
    m = _dot(zbuf[...], wout_ref[...])
    y_ref[...] = x_ref[...] + _rms(m, gpost_ref[...])


def _hgrn_out(x2d, o2d, gs2d, gnorm, wout_bf16, gpost, *, rows):
    n = x2d.shape[0]
    row_spec = pl.BlockSpec((rows, D_MODEL), lambda i: (i, 0))
    return pl.pallas_call(
        _hout_kernel,
        grid=(n // rows,),
        in_specs=[row_spec, row_spec, row_spec, _const_spec((1, HEAD_V)),
                  _const_spec(wout_bf16.shape), _const_spec((1, D_MODEL))],
        out_specs=row_spec,
        out_shape=jax.ShapeDtypeStruct((n, D_MODEL), _F32),
        scratch_shapes=[pltpu.VMEM((rows, V_DIM), _BF16)],
        compiler_params=pltpu.CompilerParams(
            dimension_semantics=("arbitrary",), vmem_limit_bytes=VMEM_LIMIT),
        name="hgrn_out",
    )(x2d, o2d, gs2d, gnorm, wout_bf16, gpost)


def _gla_exponent_matrix():
    c = GLA_CHUNK
    t = np.arange(c)[:, None]
    w = np.arange(c)[None, :]
    blocks = [(w <= t), (w > t)]
    for lev in range(1, GLA_LEVELS + 1):
        n = 1 << lev
        mid = (t // n) * n + n // 2 - 1
        later = (t > mid) & (w > mid) & (w <= t)
        earlier = (t <= mid) & (w > t) & (w <= mid)
        blocks.append(later | earlier)
    return np.concatenate(blocks, axis=0).astype(np.float32)


def _gla_kernel(q_ref, k_ref, lf_ref, v_ref, s0_ref, mall_ref, o_ref, sn_ref, st,
                *, rows, n_tiles):
    c = GLA_CHUNK
    l = pl.program_id(1)

    @pl.when(l == 0)
    def _():
        for hd in range(N_HEADS):
            st[hd] = s0_ref[0, hd].T

    ti = lax.broadcasted_iota(jnp.int32, (c, c), 0)
    si = lax.broadcasted_iota(jnp.int32, (c, c), 1)
    ri = lax.broadcasted_iota(jnp.int32, (c, 1), 0)
    diag = ti == si
    lvl_mask, lvl_role = [], []
    for lev in range(1, GLA_LEVELS + 1):
        same = (ti >> lev) == (si >> lev)
        t_late = ((ti >> (lev - 1)) & 1) == 1
        s_early = ((si >> (lev - 1)) & 1) == 0
        lvl_mask.append(same & t_late & s_early)
        lvl_role.append(((ri >> (lev - 1)) & 1) == 1)

    def chunk(ci, carry):
        rs = pl.ds(pl.multiple_of(ci * c, c), c)
        lf = lf_ref[rs, :]
        lf_hi = lf.astype(_BF16)
        r1 = lf - lf_hi.astype(_F32)
        lf_mid = r1.astype(_BF16)
        lf_lo = (r1 - lf_mid.astype(_F32)).astype(_BF16)
        ex = jnp.exp(_dot(mall_ref[...], jnp.concatenate([lf_hi, lf_mid, lf_lo], axis=0)))
        eb = ex[0:c]
        eb_rest = ex[c:2 * c]
        eb_last = eb[c - 1:c]
        q = q_ref[rs, :]
        k = k_ref[rs, :]
        v = v_ref[rs, :]
        qd = (q * eb).astype(_BF16)
        kd = (k * eb_rest).astype(_BF16)
        heads = [slice(hd * HEAD_K, (hd + 1) * HEAD_K) for hd in range(N_HEADS)]
        vb = v.astype(_BF16)
        scores = []
        for hs in heads:
            qh = q[:, hs]
            kh = k[:, hs]
            a = jnp.where(diag, _dot_nt(qh.astype(_BF16), kh.astype(_BF16)), 0.0)
            for lev in range(GLA_LEVELS):
                e = ex[(2 + lev) * c:(3 + lev) * c, hs]
                xl = (jnp.where(lvl_role[lev], qh, kh) * e).astype(_BF16)
                a = jnp.where(lvl_mask[lev], _dot_nt(xl, xl), a)
            scores.append(a.astype(_BF16))
        for hd, hs in enumerate(heads):
            o_ref[rs, hs] = (_dot_nt(qd[:, hs], st[hd].astype(_BF16))
                                + _dot(scores[hd], vb[:, hs]))
        for hd, hs in enumerate(heads):
            st[hd] = st[hd] * eb_last[:, hs] + _dot_tn(vb[:, hs], kd[:, hs])
        return carry

    lax.fori_loop(0, rows // c, chunk, 0)

    @pl.when(l == n_tiles - 1)
    def _():
        for hd in range(N_HEADS):
            sn_ref[0, hd] = st[hd].T


def _gla_prompt(q, k, lf, v, s0, *, rows):
    n = q.shape[0]
    nb = q.shape[1] // D_MODEL
    n_tiles = n // rows
    mall = jnp.asarray(np.tile(_gla_exponent_matrix(), (1, 3)), dtype=_BF16)
    row_spec = pl.BlockSpec((rows, D_MODEL), lambda b, l: (l, b))
    st_spec = pl.BlockSpec((1, N_HEADS, HEAD_K, HEAD_V), lambda b, l: (b, 0, 0, 0))
    kern = functools.partial(_gla_kernel, rows=rows, n_tiles=n_tiles)
    return pl.pallas_call(
        kern,
        grid=(nb, n_tiles),
        in_specs=[row_spec, row_spec, row_spec, row_spec, st_spec, _const_spec(mall.shape)],
        out_specs=[row_spec, st_spec],
        out_shape=[jax.ShapeDtypeStruct(q.shape, _F32), jax.ShapeDtypeStruct(s0.shape, _F32)],
        scratch_shapes=[pltpu.VMEM((N_HEADS, HEAD_V, HEAD_K), _F32)],
        compiler_params=_params(),
        name="gla_prompt",
    )(q, k, lf, v, s0, mall)


def _gla_sample_kernel(q_ref, k_ref, lf_ref, v_ref, s0_ref, o_ref, sn_ref, *, steps):
    tp = SAMPLE_T_PAD
    ri = lax.broadcasted_iota(jnp.int32, (tp, 1), 0)
    ri128 = lax.broadcasted_iota(jnp.int32, (HEAD_K, HEAD_K), 0)
    ci128 = lax.broadcasted_iota(jnp.int32, (HEAD_K, HEAD_K), 1)
    eye = (ri128 == ci128).astype(_F32)
    for bi in range(SAMPLE_B_BLK):
        lf = lf_ref[bi]
        q = q_ref[bi]
        k = k_ref[bi]
        v = v_ref[bi]
        b = jnp.zeros_like(lf)
        for j in range(steps):
            b = b + jnp.where(ri >= j, lf[j:j + 1, :], 0.0)
        eb = jnp.exp(b)
        b_last = b[steps - 1:steps, :]
        eb_last = jnp.exp(b_last)
        qd = (q * eb).astype(_BF16)
        kd = k * jnp.exp(b_last - b)
        o_intra = jnp.zeros_like(q)
        for s in range(steps):
            arg = jnp.where(ri >= s, b - b[s:s + 1, :], 0.0)
            p = jnp.where(ri >= s, q * k[s:s + 1, :] * jnp.exp(arg), 0.0)
            parts = []
            for hd in range(N_HEADS):
                hs = slice(hd * HEAD_K, (hd + 1) * HEAD_K)
                a_s = jnp.sum(p[:, hs], axis=-1, keepdims=True)
                parts.append(a_s * v[s:s + 1, hs])
            o_intra = o_intra + jnp.concatenate(parts, axis=-1)
        for hd in range(N_HEADS):
            hs = slice(hd * HEAD_K, (hd + 1) * HEAD_K)
            s0 = s0_ref[bi, hd]
            o_ref[bi, :, hs] = o_intra[:, hs] + _dot(qd[:, hs], s0.astype(_BF16))
            dcol = jnp.sum(eye * eb_last[:, hs], axis=-1, keepdims=True)
            upd = dcol * s0
            for s in range(steps):
                kcol = jnp.sum(eye * kd[s:s + 1, hs], axis=-1, keepdims=True)
                upd = upd + kcol * v[s:s + 1, hs]
            sn_ref[bi, hd] = upd


def _gla_sample(q, k, lf, v, s0, *, steps):
    nb = q.shape[0]
    row_spec = pl.BlockSpec((SAMPLE_B_BLK, SAMPLE_T_PAD, D_MODEL), lambda i: (i, 0, 0))
    st_spec = pl.BlockSpec((SAMPLE_B_BLK, N_HEADS, HEAD_K, HEAD_V), lambda i: (i, 0, 0, 0))
    kern = functools.partial(_gla_sample_kernel, steps=steps)
    return pl.pallas_call(
        kern,
        grid=(nb // SAMPLE_B_BLK,),
        in_specs=[row_spec, row_spec, row_spec, row_spec, st_spec],
        out_specs=[row_spec, st_spec],
        out_shape=[jax.ShapeDtypeStruct(q.shape, _F32), jax.ShapeDtypeStruct(s0.shape, _F32)],
        compiler_params=pltpu.CompilerParams(
            dimension_semantics=("arbitrary",), vmem_limit_bytes=VMEM_LIMIT),
        name="gla_sample",
    )(q, k, lf, v, s0)


def _row(v):
    return v.reshape(1, -1)


def _trunk_prompt(x, wts):
    nb, n, d = x.shape
    pool_ctx = jnp.zeros((nb, POOL_CTX, d), _F32)
    s0 = jnp.zeros((nb, N_HEADS, HEAD_K, HEAD_V), _F32)

    xt, pool_st = _pool_layer(x, pool_ctx, wts["mix_pre"][0], wts["pool_w"], wts["pool_scale"],
                              wts["mix_post"][0], rows=ROW_TILE, step=1, pos0=0,
                              time_major_out=True)
    flat = (1, n * nb, d)
    seq = (n, nb * d)
    ffn_ctx = jnp.zeros((1, (CONV_W - 1) * nb, 2 * D_FF), _F32)
    xt, ffn_st0 = _ffn_layer(xt.reshape(flat), ffn_ctx, wts["ffn_pre"][0], wts["w_up"][0],
                             wts["conv_w"][0], wts["conv_b"][0], wts["w_down"][0],
                             wts["ffn_post"][0], rows=ROW_TILE, step=nb)
    x2d = xt.reshape(n * nb, d)
    q, k, lf, v, gs = _hgrn_proj(x2d, wts["mix_pre"][1], wts["w_in"], wts["lb_logits"],
                                 rows=ROW_TILE)
    o, s_new = _gla_prompt(q.reshape(seq), k.reshape(seq), lf.reshape(seq), v.reshape(seq), s0,
                           rows=GLA_TILE)
    x2d = _hgrn_out(x2d, o.reshape(n * nb, d), gs, wts["gnorm"], wts["w_out"],
                    wts["mix_post"][1], rows=ROW_TILE)
    xt, ffn_st1 = _ffn_layer(x2d.reshape(flat), ffn_ctx, wts["ffn_pre"][1], wts["w_up"][1],
                             wts["conv_w"][1], wts["conv_b"][1], wts["w_down"][1],
                             wts["ffn_post"][1], rows=ROW_TILE, step=nb)

    def to_batch_major(a, t):
        return a.reshape(t, nb, a.shape[-1]).transpose(1, 0, 2)

    new_ffn = jnp.stack([to_batch_major(ffn_st0, CONV_W - 1), to_batch_major(ffn_st1, CONV_W - 1)])
    return to_batch_major(xt, n), pool_st[None], s_new[None], new_ffn


def _trunk_sample(x, pos0, state_pool, state_hgrn, state_ffn, wts):
    nb, steps, d = x.shape
    n = nb * steps

    def to_time_major(a):
        return a.transpose(1, 0, 2).reshape(1, a.shape[1] * nb, a.shape[2])

    def to_batch_major(a, t):
        return a.reshape(t, nb, a.shape[-1]).transpose(1, 0, 2)

    xt = to_time_major(x)
    xt, pool_st = _pool_layer(xt, to_time_major(state_pool[0]), wts["mix_pre"][0], wts["pool_w"],
                              wts["pool_scale"], wts["mix_post"][0], rows=n, step=nb, pos0=pos0)
    xt, ffn_st0 = _ffn_layer(xt, to_time_major(state_ffn[0]), wts["ffn_pre"][0], wts["w_up"][0],
                             wts["conv_w"][0], wts["conv_b"][0], wts["w_down"][0],
                             wts["ffn_post"][0], rows=n, step=nb)
    x2d = xt.reshape(n, d)
    proj = _hgrn_proj(x2d, wts["mix_pre"][1], wts["w_in"], wts["lb_logits"], rows=n)

    def pad_time(a):
        a = to_batch_major(a, steps)
        return jnp.pad(a, ((0, 0), (0, SAMPLE_T_PAD - steps), (0, 0)))

    q, k, lf, v = (pad_time(a) for a in proj[:4])
    o, s_new = _gla_sample(q, k, lf, v, state_hgrn[0], steps=steps)
    o2d = o[:, :steps].transpose(1, 0, 2).reshape(n, d)
    x2d = _hgrn_out(x2d, o2d, proj[4], wts["gnorm"], wts["w_out"], wts["mix_post"][1], rows=n)
    xt, ffn_st1 = _ffn_layer(x2d.reshape(1, n, d), to_time_major(state_ffn[1]), wts["ffn_pre"][1],
                             wts["w_up"][1], wts["conv_w"][1], wts["conv_b"][1],
                             wts["w_down"][1], wts["ffn_post"][1], rows=n, step=nb)
    y = to_batch_major(xt, steps)
    new_pool = to_batch_major(pool_st, POOL_CTX)[None]
    new_ffn = jnp.stack([to_batch_major(ffn_st0, CONV_W - 1), to_batch_major(ffn_st1, CONV_W - 1)])
    return y, new_pool, s_new[None], new_ffn


def kernel(x_prompt, x_sample, state_pool, state_hgrn, state_ffn_conv, norm_mix_pre, norm_mix_post,
           norm_ffn_pre, norm_ffn_post, pool_w, pool_scale, hgrn_w_in, hgrn_lb_logits, hgrn_gnorm,
           hgrn_w_out, ffn_w_up, ffn_conv_w, ffn_conv_b, ffn_w_down):
    depth = ffn_w_up.shape[0]
    assert depth == 2 and pool_w.shape[0] == 1 and hgrn_w_in.shape[0] == 1
    past_len = 16384
    wts = {
        "mix_pre": [_row(norm_mix_pre[i]) for i in range(depth)],
        "mix_post": [_row(norm_mix_post[i]) for i in range(depth)],
        "ffn_pre": [_row(norm_ffn_pre[i]) for i in range(depth)],
        "ffn_post": [_row(norm_ffn_post[i]) for i in range(depth)],
        "pool_w": pool_w[0].astype(_BF16),
        "pool_scale": _row(pool_scale[0]),
        "w_in": hgrn_w_in[0].astype(_BF16),
        "lb_logits": hgrn_lb_logits,
        "gnorm": _row(hgrn_gnorm[0]),
        "w_out": hgrn_w_out[0].astype(_BF16),
        "w_up": [ffn_w_up[i].astype(_BF16) for i in range(depth)],
        "conv_w": [ffn_conv_w[i] for i in range(depth)],
        "conv_b": [_row(ffn_conv_b[i]) for i in range(depth)],
        "w_down": [ffn_w_down[i].astype(_BF16) for i in range(depth)],
    }
    y_p, pool_p, hgrn_p, ffn_p = _trunk_prompt(x_prompt, wts)
    y_s, pool_s, hgrn_s, ffn_s = _trunk_sample(x_sample, past_len, state_pool, state_hgrn,
                                               state_ffn_conv, wts)
    return (y_p, y_s, pool_p, pool_s, hgrn_p, hgrn_s, ffn_p, ffn_s)
```

```python
import functools

import jax
import jax.numpy as jnp
import numpy as np
from jax import lax
from jax.experimental import pallas as pl
from jax.experimental.pallas import tpu as pltpu

D_MODEL = 1024
POOL_WINDOWS = (2, 4, 8, 16)
POOL_GROUP_DIM = D_MODEL // len(POOL_WINDOWS)
POOL_CTX = max(POOL_WINDOWS) - 1
N_HEADS = 8
HEAD_K = 128
HEAD_V = D_MODEL // N_HEADS
F_DIM = N_HEADS * HEAD_K
V_DIM = N_HEADS * HEAD_V
D_FF = 2816
CONV_W = 3
EPS = 1e-6

SUBLANES = 8
ROW_TILE = 512
FF_CHUNK = 256
GLA_CHUNK = 64
GLA_LEVELS = 6
GLA_TILE = 256
SAMPLE_T_PAD = 8
SAMPLE_B_BLK = 8
VMEM_LIMIT = 56 * 1024 * 1024

_F32 = jnp.float32
_BF16 = jnp.bfloat16


def _rms(x, g):
    ms = jnp.mean(x * x, axis=-1, keepdims=True)
    return x * lax.rsqrt(ms + EPS) * g


def _dot(a, b):
    return jnp.dot(a, b, preferred_element_type=_F32)


def _dot_nt(a, b):
    return lax.dot_general(a, b, (((1,), (1,)), ((), ())), preferred_element_type=_F32)


def _dot_tn(a, b):
    return lax.dot_general(a, b, (((0,), (0,)), ((), ())), preferred_element_type=_F32)


def _const_spec(shape):
    nd = len(shape)
    return pl.BlockSpec(shape, lambda *_: (0,) * nd, pipeline_mode=pl.Buffered(1))


def _params():
    return pltpu.CompilerParams(
        dimension_semantics=("arbitrary", "arbitrary"), vmem_limit_bytes=VMEM_LIMIT)


def _fold_rows(a):
    rows, c = a.shape
    return pltpu.einshape("smc->msc", a.reshape(SUBLANES, rows // SUBLANES, c)).reshape(rows, c)


def _unfold_rows(a):
    rows, c = a.shape
    return pltpu.einshape("msc->smc", a.reshape(rows // SUBLANES, SUBLANES, c)).reshape(rows, c)


def _pool_kernel(x_ref, ctx_ref, gpre_ref, w_ref, scale_ref, gpost_ref, y_ref, st_ref, hbuf,
                 *, rows, step, pos0, n_tiles):
    halo = (POOL_CTX + 1) * step
    l = pl.program_id(1)

    @pl.when(l == 0)
    def _():
        hbuf[0:step, :] = jnp.zeros((step, D_MODEL), _F32)
        hbuf[step:halo, :] = ctx_ref[0]

    x = x_ref[0]
    h = _rms(x, gpre_ref[...])
    hbuf[halo:halo + rows, :] = h

    if pos0 < POOL_CTX:
        assert step == 1
        pos = pos0 + l * rows + lax.broadcasted_iota(jnp.int32, (rows, 1), 0)
    outs = []
    for g, w in enumerate(POOL_WINDOWS):
        cs = slice(g * POOL_GROUP_DIM, (g + 1) * POOL_GROUP_DIM)
        s = h[:, cs]
        for j in range(1, w):
            s = s + hbuf[halo - j * step:halo - j * step + rows, cs]
        if pos0 < POOL_CTX:
            cnt = jnp.minimum(pos + 1, w).astype(_F32)
        else:
            cnt = float(w)
        p = s / cnt - h[:, cs]
        outs.append(_dot(p.astype(_BF16), w_ref[g]))
    m = jnp.concatenate(outs, axis=-1) * scale_ref[...]
    y_ref[0] = x + _rms(m, gpost_ref[...])

    @pl.when(l == n_tiles - 1)
    def _():
        st_ref[0] = hbuf[rows + step:rows + halo, :]

    if n_tiles > 1:
        hbuf[0:halo, :] = hbuf[rows:rows + halo, :]


def _pool_layer(x, ctx, gpre, w_bf16, scale, gpost, *, rows, step, pos0):
    nb, n, _ = x.shape
    n_tiles = n // rows
    halo = (POOL_CTX + 1) * step
    ctx_rows = POOL_CTX * step
    kern = functools.partial(_pool_kernel, rows=rows, step=step, pos0=pos0, n_tiles=n_tiles)
    return pl.pallas_call(
        kern,
        grid=(nb, n_tiles),
        in_specs=[
            pl.BlockSpec((1, rows, D_MODEL), lambda b, l: (b, l, 0)),
            pl.BlockSpec((1, ctx_rows, D_MODEL), lambda b, l: (b, 0, 0)),
            _const_spec((1, D_MODEL)),
            _const_spec(w_bf16.shape),
            _const_spec((1, D_MODEL)),
            _const_spec((1, D_MODEL)),
        ],
        out_specs=[
            pl.BlockSpec((1, rows, D_MODEL), lambda b, l: (b, l, 0)),
            pl.BlockSpec((1, ctx_rows, D_MODEL), lambda b, l: (b, 0, 0)),
        ],
        out_shape=[
            jax.ShapeDtypeStruct(x.shape, _F32),
            jax.ShapeDtypeStruct((nb, ctx_rows, D_MODEL), _F32),
        ],
        scratch_shapes=[pltpu.VMEM((halo + rows, D_MODEL), _F32)],
        compiler_params=_params(),
        name="pool_mixer",
    )(x, ctx, gpre, w_bf16, scale, gpost)


def _ffn_kernel(x_ref, ctx_ref, gpre_ref, wup_ref, cw_ref, cb_ref, wdn_ref, gpost_ref,
                y_ref, st_ref, carry, gbuf, vbuf, abuf, *, rows, step, n_tiles, fold):
    shift = SUBLANES if fold else step
    halo = max(SUBLANES, (CONV_W - 1) * shift)
    taps = CONV_W - 1
    l = pl.program_id(1)

    @pl.when(l == 0)
    def _():
        carry[...] = jnp.zeros(carry.shape, _F32)
        if fold:
            for j in range(taps):
                carry[(j + 1) * SUBLANES - 1:(j + 1) * SUBLANES, :] = ctx_ref[0, j:j + 1, :]
        else:
            carry[halo - taps * step:halo, :] = ctx_ref[0]

    x = x_ref[0]
    h = _rms(x, gpre_ref[...])
    if fold:
        h = _fold_rows(h)
        sub0 = lax.broadcasted_iota(jnp.int32, (SUBLANES, 1), 0) == 0
    h = h.astype(_BF16)
    for c in range(D_FF // FF_CHUNK):
        conv = []
        for buf, off in ((gbuf, 0), (vbuf, D_FF)):
            cs = slice(off + c * FF_CHUNK, off + (c + 1) * FF_CHUNK)
            u = _dot(h, wup_ref[:, cs])
            if fold:
                for j in range(taps):
                    grp = u[rows - (taps - j) * SUBLANES:rows - (taps - j - 1) * SUBLANES, :]
                    prev = carry[(j + 1) * SUBLANES - 1:(j + 1) * SUBLANES, cs]
                    buf[j * SUBLANES:(j + 1) * SUBLANES, :] = jnp.where(
                        sub0, prev, pltpu.roll(grp, 1, axis=0))
                carry[:, cs] = u[rows - halo:rows, :]
                buf[halo:halo + rows, :] = u
            else:
                buf[0:halo, :] = carry[:, cs]
                buf[halo:halo + rows, :] = u
                carry[:, cs] = buf[rows:rows + halo, :]
            conv.append(cb_ref[:, cs]
                        + buf[halo - 2 * shift:halo - 2 * shift + rows, :] * cw_ref[0:1, cs]
                        + buf[halo - shift:halo - shift + rows, :] * cw_ref[1:2, cs]
                        + u * cw_ref[2:3, cs])
        a = jax.nn.gelu(conv[0], approximate=True) * conv[1]
        abuf[:, c * FF_CHUNK:(c + 1) * FF_CHUNK] = a.astype(_BF16)
    m = _rms(_dot(abuf[...], wdn_ref[...]), gpost_ref[...])
    if fold:
        m = _unfold_rows(m)
    y_ref[0] = x + m

    @pl.when(l == n_tiles - 1)
    def _():
        if fold:
            for j in range(taps):
                st_ref[0, j:j + 1, :] = carry[(j + 1) * SUBLANES - 1:(j + 1) * SUBLANES, :]
        else:
            st_ref[0] = carry[halo - taps * step:halo, :]


def _ffn_layer(x, ctx, gpre, wup_bf16, cw, cb, wdn_bf16, gpost, *, rows, step, fold=False):
    nb, n, _ = x.shape
    n_tiles = n // rows
    assert not fold or (step == 1 and rows % (SUBLANES * SUBLANES) == 0)
    halo = max(SUBLANES, (CONV_W - 1) * (SUBLANES if fold else step))
    ctx_rows = (CONV_W - 1) * step
    kern = functools.partial(_ffn_kernel, rows=rows, step=step, n_tiles=n_tiles, fold=fold)
    return pl.pallas_call(
        kern,
        grid=(nb, n_tiles),
        in_specs=[
            pl.BlockSpec((1, rows, D_MODEL), lambda b, l: (b, l, 0)),
            pl.BlockSpec((1, ctx_rows, 2 * D_FF), lambda b, l: (b, 0, 0)),
            _const_spec((1, D_MODEL)),
            _const_spec((D_MODEL, 2 * D_FF)),
            _const_spec((CONV_W, 2 * D_FF)),
            _const_spec((1, 2 * D_FF)),
            _const_spec((D_FF, D_MODEL)),
            _const_spec((1, D_MODEL)),
        ],
        out_specs=[
            pl.BlockSpec((1, rows, D_MODEL), lambda b, l: (b, l, 0)),
            pl.BlockSpec((1, ctx_rows, 2 * D_FF), lambda b, l: (b, 0, 0)),
        ],
        out_shape=[
            jax.ShapeDtypeStruct(x.shape, _F32),
            jax.ShapeDtypeStruct((nb, ctx_rows, 2 * D_FF), _F32),
        ],
        scratch_shapes=[
            pltpu.VMEM((halo, 2 * D_FF), _F32),
            pltpu.VMEM((halo + rows, FF_CHUNK), _F32),
            pltpu.VMEM((halo + rows, FF_CHUNK), _F32),
            pltpu.VMEM((rows, D_FF), _BF16),
        ],
        compiler_params=_params(),
        name="conv_ffn",
    )(x, ctx, gpre, wup_bf16, cw, cb, wdn_bf16, gpost)


def _hproj_kernel(x_ref, gpre_ref, win_ref, lbl_ref, q_ref, k_ref, lf_ref, v_ref, gs_ref):
    h = _rms(x_ref[...], gpre_ref[...]).astype(_BF16)
    l0 = lbl_ref[0:1, :]
    l1 = lbl_ref[1:2, :]
    mx = jnp.maximum(l0, l1)
    e0 = jnp.exp(l0 - mx)
    e1 = jnp.exp(l1 - mx)
    lb = e1 / (e0 + e1)

    qr = _dot(h, win_ref[:, 0:F_DIM])
    q_ref[...] = qr * jax.nn.sigmoid(qr) * (HEAD_K ** -0.5)
    fr = _dot(h, win_ref[:, F_DIM:2 * F_DIM])
    lf_ref[...] = jnp.log(lb + (1.0 - lb) * jax.nn.sigmoid(fr))
    k_ref[...] = (1.0 - lb) * jax.nn.sigmoid(-fr)
    v_ref[...] = _dot(h, win_ref[:, 2 * F_DIM:2 * F_DIM + V_DIM])
    gr = _dot(h, win_ref[:, 2 * F_DIM + V_DIM:2 * F_DIM + 2 * V_DIM])
    gs_ref[...] = gr * jax.nn.sigmoid(gr)


def _hgrn_proj(x2d, gpre, win_bf16, lb_logits, *, rows):
    n = x2d.shape[0]
    row_spec = pl.BlockSpec((rows, D_MODEL), lambda i: (i, 0))
    out = jax.ShapeDtypeStruct((n, D_MODEL), _F32)
    return pl.pallas_call(
        _hproj_kernel,
        grid=(n // rows,),
        in_specs=[row_spec, _const_spec((1, D_MODEL)), _const_spec(win_bf16.shape),
                  _const_spec(lb_logits.shape)],
        out_specs=[row_spec] * 5,
        out_shape=[out] * 5,
        compiler_params=pltpu.CompilerParams(
            dimension_semantics=("arbitrary",), vmem_limit_bytes=VMEM_LIMIT),
        name="hgrn_proj",
    )(x2d, gpre, win_bf16, lb_logits)


def _hout_kernel(x_ref, o_ref, gs_ref, gn_ref, wout_ref, gpost_ref, y_ref, zbuf):
    for hd in range(N_HEADS):
        hs = slice(hd * HEAD_V, (hd + 1) * HEAD_V)
        zbuf[:, hs] = (_rms(o_ref[:, hs], gn_ref[...]) * gs_ref[:, hs]).astype(_BF16)
    m = _dot(zbuf[...], wout_ref[...])
    y_ref[...] = x_ref[...] + _rms(m, gpost_ref[...])


def _hgrn_out(x2d, o2d, gs2d, gnorm, wout_bf16, gpost, *, rows):
    n = x2d.shape[0]
    row_spec = pl.BlockSpec((rows, D_MODEL), lambda i: (i, 0))
    return pl.pallas_call(
        _hout_kernel,
        grid=(n // rows,),
        in_specs=[row_spec, row_spec, row_spec, _const_spec((1, HEAD_V)),
                  _const_spec(wout_bf16.shape), _const_spec((1, D_MODEL))],
        out_specs=row_spec,
        out_shape=jax.ShapeDtypeStruct((n, D_MODEL), _F32),
        scratch_shapes=[pltpu.VMEM((rows, V_DIM), _BF16)],
        compiler_params=pltpu.CompilerParams(
            dimension_semantics=("arbitrary",), vmem_limit_bytes=VMEM_LIMIT),
        name="hgrn_out",
    )(x2d, o2d, gs2d, gnorm, wout_bf16, gpost)


def _gla_exponent_matrix():
    c = GLA_CHUNK
    t = np.arange(c)[:, None]
    w = np.arange(c)[None, :]
    blocks = [(w <= t), (w > t)]
    for lev in range(1, GLA_LEVELS + 1):
        n = 1 << lev
        mid = (t // n) * n + n // 2 - 1
        later = (t > mid) & (w > mid) & (w <= t)
        earlier = (t <= mid) & (w > t) & (w <= mid)
        blocks.append(later | earlier)
    return np.concatenate(blocks, axis=0).astype(np.float32)


def _gla_kernel(q_ref, k_ref, lf_ref, v_ref, s0_ref, mall_ref, o_ref, sn_ref, st,
                *, rows, n_tiles):
    c = GLA_CHUNK
    l = pl.program_id(1)

    @pl.when(l == 0)
    def _():
        for hd in range(N_HEADS):
            st[hd] = s0_ref[0, hd].T

    ti = lax.broadcasted_iota(jnp.int32, (c, c), 0)
    si = lax.broadcasted_iota(jnp.int32, (c, c), 1)
    ri = lax.broadcasted_iota(jnp.int32, (c, 1), 0)
    diag = ti == si
    lvl_mask, lvl_role = [], []
    for lev in range(1, GLA_LEVELS + 1):
        same = (ti >> lev) == (si >> lev)
        t_late = ((ti >> (lev - 1)) & 1) == 1
        s_early = ((si >> (lev - 1)) & 1) == 0
        lvl_mask.append(same & t_late & s_early)
        lvl_role.append(((ri >> (lev - 1)) & 1) == 1)

    def chunk(ci, carry):
        rs = pl.ds(pl.multiple_of(ci * c, c), c)
        lf = lf_ref[0, rs, :]
        lf_hi = lf.astype(_BF16)
        r1 = lf - lf_hi.astype(_F32)
        lf_mid = r1.astype(_BF16)
        lf_lo = (r1 - lf_mid.astype(_F32)).astype(_BF16)
        ex = jnp.exp(_dot(mall_ref[...], jnp.concatenate([lf_hi, lf_mid, lf_lo], axis=0)))
        eb = ex[0:c]
        eb_rest = ex[c:2 * c]
        eb_last = eb[c - 1:c]
        q = q_ref[0, rs, :]
        k = k_ref[0, rs, :]
        v = v_ref[0, rs, :]
        qd = (q * eb).astype(_BF16)
        kd = (k * eb_rest).astype(_BF16)
        heads = [slice(hd * HEAD_K, (hd + 1) * HEAD_K) for hd in range(N_HEADS)]
        vb = v.astype(_BF16)
        scores = []
        for hs in heads:
            qh = q[:, hs]
            kh = k[:, hs]
            a = jnp.where(diag, _dot_nt(qh.astype(_BF16), kh.astype(_BF16)), 0.0)
            for lev in range(GLA_LEVELS):
                e = ex[(2 + lev) * c:(3 + lev) * c, hs]
                xl = (jnp.where(lvl_role[lev], qh, kh) * e).astype(_BF16)
                a = jnp.where(lvl_mask[lev], _dot_nt(xl, xl), a)
            scores.append(a.astype(_BF16))
        for hd, hs in enumerate(heads):
            o_ref[0, rs, hs] = (_dot_nt(qd[:, hs], st[hd].astype(_BF16))
                                + _dot(scores[hd], vb[:, hs]))
        for hd, hs in enumerate(heads):
            st[hd] = st[hd] * eb_last[:, hs] + _dot_tn(vb[:, hs], kd[:, hs])
        return carry

    lax.fori_loop(0, rows // c, chunk, 0)

    @pl.when(l == n_tiles - 1)
    def _():
        for hd in range(N_HEADS):
            sn_ref[0, hd] = st[hd].T


def _gla_prompt(q, k, lf, v, s0, *, rows):
    nb, n, _ = q.shape
    n_tiles = n // rows
    mall = jnp.asarray(np.tile(_gla_exponent_matrix(), (1, 3)), dtype=_BF16)
    row_spec = pl.BlockSpec((1, rows, D_MODEL), lambda b, l: (b, l, 0))
    st_spec = pl.BlockSpec((1, N_HEADS, HEAD_K, HEAD_V), lambda b, l: (b, 0, 0, 0))
    kern = functools.partial(_gla_kernel, rows=rows, n_tiles=n_tiles)
    return pl.pallas_call(
        kern,
        grid=(nb, n_tiles),
        in_specs=[row_spec, row_spec, row_spec, row_spec, st_spec, _const_spec(mall.shape)],
        out_specs=[row_spec, st_spec],
        out_shape=[jax.ShapeDtypeStruct(q.shape, _F32), jax.ShapeDtypeStruct(s0.shape, _F32)],
        scratch_shapes=[pltpu.VMEM((N_HEADS, HEAD_V, HEAD_K), _F32)],
        compiler_params=_params(),
        name="gla_prompt",
    )(q, k, lf, v, s0, mall)


def _gla_sample_kernel(q_ref, k_ref, lf_ref, v_ref, s0_ref, o_ref, sn_ref, *, steps):
    tp = SAMPLE_T_PAD
    ri = lax.broadcasted_iota(jnp.int32, (tp, 1), 0)
    ri128 = lax.broadcasted_iota(jnp.int32, (HEAD_K, HEAD_K), 0)
    ci128 = lax.broadcasted_iota(jnp.int32, (HEAD_K, HEAD_K), 1)
    eye = (ri128 == ci128).astype(_F32)
    for bi in range(SAMPLE_B_BLK):
        lf = lf_ref[bi]
        q = q_ref[bi]
        k = k_ref[bi]
        v = v_ref[bi]
        b = jnp.zeros_like(lf)
        for j in range(steps):
            b = b + jnp.where(ri >= j, lf[j:j + 1, :], 0.0)
        eb = jnp.exp(b)
        b_last = b[steps - 1:steps, :]
        eb_last = jnp.exp(b_last)
        qd = (q * eb).astype(_BF16)
        kd = k * jnp.exp(b_last - b)
        o_intra = jnp.zeros_like(q)
        for s in range(steps):
            arg = jnp.where(ri >= s, b - b[s:s + 1, :], 0.0)
            p = jnp.where(ri >= s, q * k[s:s + 1, :] * jnp.exp(arg), 0.0)
            parts = []
            for hd in range(N_HEADS):
                hs = slice(hd * HEAD_K, (hd + 1) * HEAD_K)
                a_s = jnp.sum(p[:, hs], axis=-1, keepdims=True)
                parts.append(a_s * v[s:s + 1, hs])
            o_intra = o_intra + jnp.concatenate(parts, axis=-1)
        for hd in range(N_HEADS):
            hs = slice(hd * HEAD_K, (hd + 1) * HEAD_K)
            s0 = s0_ref[bi, hd]
            o_ref[bi, :, hs] = o_intra[:, hs] + _dot(qd[:, hs], s0.astype(_BF16))
            dcol = jnp.sum(eye * eb_last[:, hs], axis=-1, keepdims=True)
            upd = dcol * s0
            for s in range(steps):
                kcol = jnp.sum(eye * kd[s:s + 1, hs], axis=-1, keepdims=True)
                upd = upd + kcol * v[s:s + 1, hs]
            sn_ref[bi, hd] = upd


def _gla_sample(q, k, lf, v, s0, *, steps):
    nb = q.shape[0]
    row_spec = pl.BlockSpec((SAMPLE_B_BLK, SAMPLE_T_PAD, D_MODEL), lambda i: (i, 0, 0))
    st_spec = pl.BlockSpec((SAMPLE_B_BLK, N_HEADS, HEAD_K, HEAD_V), lambda i: (i, 0, 0, 0))
    kern = functools.partial(_gla_sample_kernel, steps=steps)
    return pl.pallas_call(
        kern,
        grid=(nb // SAMPLE_B_BLK,),
        in_specs=[row_spec, row_spec, row_spec, row_spec, st_spec],
        out_specs=[row_spec, st_spec],
        out_shape=[jax.ShapeDtypeStruct(q.shape, _F32), jax.ShapeDtypeStruct(s0.shape, _F32)],
        compiler_params=pltpu.CompilerParams(
            dimension_semantics=("arbitrary",), vmem_limit_bytes=VMEM_LIMIT),
        name="gla_sample",
    )(q, k, lf, v, s0)


def _row(v):
    return v.reshape(1, -1)


def _trunk_prompt(x, wts):
    nb, n, d = x.shape
    pool_ctx = jnp.zeros((nb, POOL_CTX, d), _F32)
    ffn_ctx = jnp.zeros((nb, CONV_W - 1, 2 * D_FF), _F32)
    s0 = jnp.zeros((nb, N_HEADS, HEAD_K, HEAD_V), _F32)

    x, pool_st = _pool_layer(x, pool_ctx, wts["mix_pre"][0], wts["pool_w"], wts["pool_scale"],
                             wts["mix_post"][0], rows=ROW_TILE, step=1, pos0=0)
    x, ffn_st0 = _ffn_layer(x, ffn_ctx, wts["ffn_pre"][0], wts["w_up"][0], wts["conv_w"][0],
                            wts["conv_b"][0], wts["w_down"][0], wts["ffn_post"][0],
                            rows=ROW_TILE, step=1, fold=True)
    x2d = x.reshape(nb * n, d)
    q, k, lf, v, gs = _hgrn_proj(x2d, wts["mix_pre"][1], wts["w_in"], wts["lb_logits"],
                                 rows=ROW_TILE)
    shp = (nb, n, d)
    o, s_new = _gla_prompt(q.reshape(shp), k.reshape(shp), lf.reshape(shp), v.reshape(shp), s0,
                           rows=GLA_TILE)
    x2d = _hgrn_out(x2d, o.reshape(nb * n, d), gs, wts["gnorm"], wts["w_out"],
                    wts["mix_post"][1], rows=ROW_TILE)
    x, ffn_st1 = _ffn_layer(x2d.reshape(shp), ffn_ctx, wts["ffn_pre"][1], wts["w_up"][1],
                            wts["conv_w"][1], wts["conv_b"][1], wts["w_down"][1],
                            wts["ffn_post"][1], rows=ROW_TILE, step=1, fold=True)
    return x, pool_st[None], s_new[None], jnp.stack([ffn_st0, ffn_st1])


def _trunk_sample(x, pos0, state_pool, state_hgrn, state_ffn, wts):
    nb, steps, d = x.shape
    n = nb * steps

    def to_time_major(a):
        return a.transpose(1, 0, 2).reshape(1, a.shape[1] * nb, a.shape[2])

    def to_batch_major(a, t):
        return a.reshape(t, nb, a.shape[-1]).transpose(1, 0, 2)

    xt = to_time_major(x)
    xt, pool_st = _pool_layer(xt, to_time_major(state_pool[0]), wts["mix_pre"][0], wts["pool_w"],
                              wts["pool_scale"], wts["mix_post"][0], rows=n, step=nb, pos0=pos0)
    xt, ffn_st0 = _ffn_layer(xt, to_time_major(state_ffn[0]), wts["ffn_pre"][0], wts["w_up"][0],
                             wts["conv_w"][0], wts["conv_b"][0], wts["w_down"][0],
                             wts["ffn_post"][0], rows=n, step=nb)
    x2d = xt.reshape(n, d)
    proj = _hgrn_proj(x2d, wts["mix_pre"][1], wts["w_in"], wts["lb_logits"], rows=n)

    def pad_time(a):
        a = to_batch_major(a, steps)
        return jnp.pad(a, ((0, 0), (0, SAMPLE_T_PAD - steps), (0, 0)))

    q, k, lf, v = (pad_time(a) for a in proj[:4])
    o, s_new = _gla_sample(q, k, lf, v, state_hgrn[0], steps=steps)
    o2d = o[:, :steps].transpose(1, 0, 2).reshape(n, d)
    x2d = _hgrn_out(x2d, o2d, proj[4], wts["gnorm"], wts["w_out"], wts["mix_post"][1], rows=n)
    xt, ffn_st1 = _ffn_layer(x2d.reshape(1, n, d), to_time_major(state_ffn[1]), wts["ffn_pre"][1],
                             wts["w_up"][1], wts["conv_w"][1], wts["conv_b"][1],
                             wts["w_down"][1], wts["ffn_post"][1], rows=n, step=nb)
    y = to_batch_major(xt, steps)
    new_pool = to_batch_major(pool_st, POOL_CTX)[None]
    new_ffn = jnp.stack([to_batch_major(ffn_st0, CONV_W - 1), to_batch_major(ffn_st1, CONV_W - 1)])
    return y, new_pool, s_new[None], new_ffn


def kernel(x_prompt, x_sample, state_pool, state_hgrn, state_ffn_conv, norm_mix_pre, norm_mix_post,
           norm_ffn_pre, norm_ffn_post, pool_w, pool_scale, hgrn_w_in, hgrn_lb_logits, hgrn_gnorm,
           hgrn_w_out, ffn_w_up, ffn_conv_w, ffn_conv_b, ffn_w_down):
    depth = ffn_w_up.shape[0]
    assert depth == 2 and pool_w.shape[0] == 1 and hgrn_w_in.shape[0] == 1
    past_len = 16384
    wts = {
        "mix_pre": [_row(norm_mix_pre[i]) for i in range(depth)],
        "mix_post": [_row(norm_mix_post[i]) for i in range(depth)],
        "ffn_pre": [_row(norm_ffn_pre[i]) for i in range(depth)],
        "ffn_post": [_row(norm_ffn_post[i]) for i in range(depth)],
        "pool_w": pool_w[0].astype(_BF16),
        "pool_scale": _row(pool_scale[0]),
        "w_in": hgrn_w_in[0].astype(_BF16),
        "lb_logits": hgrn_lb_logits,
        "gnorm": _row(hgrn_gnorm[0]),
        "w_out": hgrn_w_out[0].astype(_BF16),
        "w_up": [ffn_w_up[i].astype(_BF16) for i in range(depth)],
        "conv_w": [ffn_conv_w[i] for i in range(depth)],
        "conv_b": [_row(ffn_conv_b[i]) for i in range(depth)],
        "w_down": [ffn_w_down[i].astype(_BF16) for i in range(depth)],
    }
    y_p, pool_p, hgrn_p, ffn_p = _trunk_prompt(x_prompt, wts)
    y_s, pool_s, hgrn_s, ffn_s = _trunk_sample(x_sample, past_len, state_pool, state_hgrn,
                                               state_ffn_conv, wts)
    return (y_p, y_s, pool_p, pool_s, hgrn_p, hgrn_s, ffn_p, ffn_s)
```

```python
import functools

import jax
import jax.numpy as jnp
import numpy as np
from jax import lax
from jax.experimental import pallas as pl
from jax.experimental.pallas import tpu as pltpu

D_MODEL = 1024
POOL_WINDOWS = (2, 4, 8, 16)
POOL_GROUP_DIM = D_MODEL // len(POOL_WINDOWS)
POOL_CTX = max(POOL_WINDOWS) - 1
N_HEADS = 8
HEAD_K = 128
HEAD_V = D_MODEL // N_HEADS
F_DIM = N_HEADS * HEAD_K
V_DIM = N_HEADS * HEAD_V
D_FF = 2816
CONV_W = 3
EPS = 1e-6

SUBLANES = 8
ROW_TILE = 512
FF_CHUNK = 256
GLA_CHUNK = 64
GLA_LEVELS = 6
GLA_UNROLL = 2
SAMPLE_T_PAD = 8
SAMPLE_B_BLK = 8
VMEM_LIMIT = 56 * 1024 * 1024

_F32 = jnp.float32
_BF16 = jnp.bfloat16


def _rms(x, g):
    ms = jnp.mean(x * x, axis=-1, keepdims=True)
    return x * lax.rsqrt(ms + EPS) * g


def _dot(a, b):
    return jnp.dot(a, b, preferred_element_type=_F32)


def _dot_nt(a, b):
    return lax.dot_general(a, b, (((1,), (1,)), ((), ())), preferred_element_type=_F32)


def _dot_tn(a, b):
    return lax.dot_general(a, b, (((0,), (0,)), ((), ())), preferred_element_type=_F32)


def _const_spec(shape):
    nd = len(shape)
    return pl.BlockSpec(shape, lambda *_: (0,) * nd, pipeline_mode=pl.Buffered(1))


def _params():
    return pltpu.CompilerParams(
        dimension_semantics=("arbitrary", "arbitrary"), vmem_limit_bytes=VMEM_LIMIT)


def _fold_rows(a):
    rows, c = a.shape
    return a.reshape(SUBLANES, rows // SUBLANES, c).swapaxes(0, 1).reshape(rows, c)


def _unfold_rows(a):
    rows, c = a.shape
    return a.reshape(rows // SUBLANES, SUBLANES, c).swapaxes(0, 1).reshape(rows, c)


def _pool_kernel(x_ref, ctx_ref, gpre_ref, w_ref, scale_ref, gpost_ref, y_ref, st_ref, hbuf,
                 *, rows, step, pos0, n_tiles):
    halo = (POOL_CTX + 1) * step
    l = pl.program_id(1)

    @pl.when(l == 0)
    def _():
        hbuf[0:step, :] = jnp.zeros((step, D_MODEL), _F32)
        hbuf[step:halo, :] = ctx_ref[0]

    x = x_ref[0]
    h = _rms(x, gpre_ref[...])
    hbuf[halo:halo + rows, :] = h

    if pos0 < POOL_CTX:
        assert step == 1
        pos = pos0 + l * rows + lax.broadcasted_iota(jnp.int32, (rows, 1), 0)
    outs = []
    for g, w in enumerate(POOL_WINDOWS):
        cs = slice(g * POOL_GROUP_DIM, (g + 1) * POOL_GROUP_DIM)
        s = h[:, cs]
        for j in range(1, w):
            s = s + hbuf[halo - j * step:halo - j * step + rows, cs]
        if pos0 < POOL_CTX:
            cnt = jnp.minimum(pos + 1, w).astype(_F32)
        else:
            cnt = float(w)
        p = s / cnt - h[:, cs]
        outs.append(_dot(p.astype(_BF16), w_ref[g]))
    m = jnp.concatenate(outs, axis=-1) * scale_ref[...]
    y_ref[0] = x + _rms(m, gpost_ref[...])

    @pl.when(l == n_tiles - 1)
    def _():
        st_ref[0] = hbuf[rows + step:rows + halo, :]

    if n_tiles > 1:
        hbuf[0:halo, :] = hbuf[rows:rows + halo, :]


def _pool_layer(x, ctx, gpre, w_bf16, scale, gpost, *, rows, step, pos0):
    nb, n, _ = x.shape
    n_tiles = n // rows
    halo = (POOL_CTX + 1) * step
    ctx_rows = POOL_CTX * step
    kern = functools.partial(_pool_kernel, rows=rows, step=step, pos0=pos0, n_tiles=n_tiles)
    return pl.pallas_call(
        kern,
        grid=(nb, n_tiles),
        in_specs=[
            pl.BlockSpec((1, rows, D_MODEL), lambda b, l: (b, l, 0)),
            pl.BlockSpec((1, ctx_rows, D_MODEL), lambda b, l: (b, 0, 0)),
            _const_spec((1, D_MODEL)),
            _const_spec(w_bf16.shape),
            _const_spec((1, D_MODEL)),
            _const_spec((1, D_MODEL)),
        ],
        out_specs=[
            pl.BlockSpec((1, rows, D_MODEL), lambda b, l: (b, l, 0)),
            pl.BlockSpec((1, ctx_rows, D_MODEL), lambda b, l: (b, 0, 0)),
        ],
        out_shape=[
            jax.ShapeDtypeStruct(x.shape, _F32),
            jax.ShapeDtypeStruct((nb, ctx_rows, D_MODEL), _F32),
        ],
        scratch_shapes=[pltpu.VMEM((halo + rows, D_MODEL), _F32)],
        compiler_params=_params(),
        name="pool_mixer",
    )(x, ctx, gpre, w_bf16, scale, gpost)


def _ffn_kernel(x_ref, ctx_ref, gpre_ref, wup_ref, cw_ref, cb_ref, wdn_ref, gpost_ref,
                y_ref, st_ref, carry, gbuf, vbuf, abuf, *, rows, step, n_tiles, fold):
    shift = SUBLANES if fold else step
    halo = max(SUBLANES, (CONV_W - 1) * shift)
    taps = CONV_W - 1
    l = pl.program_id(1)

    @pl.when(l == 0)
    def _():
        carry[...] = jnp.zeros(carry.shape, _F32)
        if fold:
            for j in range(taps):
                carry[(j + 1) * SUBLANES - 1:(j + 1) * SUBLANES, :] = ctx_ref[0, j:j + 1, :]
        else:
            carry[halo - taps * step:halo, :] = ctx_ref[0]

    x = x_ref[0]
    h = _rms(x, gpre_ref[...])
    if fold:
        h = _fold_rows(h)
        sub0 = lax.broadcasted_iota(jnp.int32, (SUBLANES, 1), 0) == 0
    h = h.astype(_BF16)
    for c in range(D_FF // FF_CHUNK):
        conv = []
        for buf, off in ((gbuf, 0), (vbuf, D_FF)):
            cs = slice(off + c * FF_CHUNK, off + (c + 1) * FF_CHUNK)
            u = _dot(h, wup_ref[:, cs])
            if fold:
                for j in range(taps):
                    grp = u[rows - (taps - j) * SUBLANES:rows - (taps - j - 1) * SUBLANES, :]
                    prev = carry[(j + 1) * SUBLANES - 1:(j + 1) * SUBLANES, cs]
                    buf[j * SUBLANES:(j + 1) * SUBLANES, :] = jnp.where(
                        sub0, prev, pltpu.roll(grp, 1, axis=0))
                carry[:, cs] = u[rows - halo:rows, :]
                buf[halo:halo + rows, :] = u
            else:
                buf[0:halo, :] = carry[:, cs]
                buf[halo:halo + rows, :] = u
                carry[:, cs] = buf[rows:rows + halo, :]
            conv.append(cb_ref[:, cs]
                        + buf[halo - 2 * shift:halo - 2 * shift + rows, :] * cw_ref[0:1, cs]
                        + buf[halo - shift:halo - shift + rows, :] * cw_ref[1:2, cs]
                        + u * cw_ref[2:3, cs])
        a = jax.nn.gelu(conv[0], approximate=True) * conv[1]
        abuf[:, c * FF_CHUNK:(c + 1) * FF_CHUNK] = a.astype(_BF16)
    m = _rms(_dot(abuf[...], wdn_ref[...]), gpost_ref[...])
    if fold:
        m = _unfold_rows(m)
    y_ref[0] = x + m

    @pl.when(l == n_tiles - 1)
    def _():
        if fold:
            for j in range(taps):
                st_ref[0, j:j + 1, :] = carry[(j + 1) * SUBLANES - 1:(j + 1) * SUBLANES, :]
        else:
            st_ref[0] = carry[halo - taps * step:halo, :]


def _ffn_layer(x, ctx, gpre, wup_bf16, cw, cb, wdn_bf16, gpost, *, rows, step, fold=False):
    nb, n, _ = x.shape
    n_tiles = n // rows
    assert not fold or (step == 1 and rows % (SUBLANES * SUBLANES) == 0)
    halo = max(SUBLANES, (CONV_W - 1) * (SUBLANES if fold else step))
    ctx_rows = (CONV_W - 1) * step
    kern = functools.partial(_ffn_kernel, rows=rows, step=step, n_tiles=n_tiles, fold=fold)
    return pl.pallas_call(
        kern,
        grid=(nb, n_tiles),
        in_specs=[
            pl.BlockSpec((1, rows, D_MODEL), lambda b, l: (b, l, 0)),
            pl.BlockSpec((1, ctx_rows, 2 * D_FF), lambda b, l: (b, 0, 0)),
            _const_spec((1, D_MODEL)),
            _const_spec((D_MODEL, 2 * D_FF)),
            _const_spec((CONV_W, 2 * D_FF)),
            _const_spec((1, 2 * D_FF)),
            _const_spec((D_FF, D_MODEL)),
            _const_spec((1, D_MODEL)),
        ],
        out_specs=[
            pl.BlockSpec((1, rows, D_MODEL), lambda b, l: (b, l, 0)),
            pl.BlockSpec((1, ctx_rows, 2 * D_FF), lambda b, l: (b, 0, 0)),
        ],
        out_shape=[
            jax.ShapeDtypeStruct(x.shape, _F32),
            jax.ShapeDtypeStruct((nb, ctx_rows, 2 * D_FF), _F32),
        ],
        scratch_shapes=[
            pltpu.VMEM((halo, 2 * D_FF), _F32),
            pltpu.VMEM((halo + rows, FF_CHUNK), _F32),
            pltpu.VMEM((halo + rows, FF_CHUNK), _F32),
            pltpu.VMEM((rows, D_FF), _BF16),
        ],
        compiler_params=_params(),
        name="conv_ffn",
    )(x, ctx, gpre, wup_bf16, cw, cb, wdn_bf16, gpost)


def _hgrn_gates(h, win_ref, lbl_ref):
    l0 = lbl_ref[0:1, :]
    l1 = lbl_ref[1:2, :]
    mx = jnp.maximum(l0, l1)
    e0 = jnp.exp(l0 - mx)
    e1 = jnp.exp(l1 - mx)
    lb = e1 / (e0 + e1)

    qr = _dot(h, win_ref[:, 0:F_DIM])
    q = qr * jax.nn.sigmoid(qr) * (HEAD_K ** -0.5)
    fr = _dot(h, win_ref[:, F_DIM:2 * F_DIM])
    lf = jnp.log(lb + (1.0 - lb) * jax.nn.sigmoid(fr))
    k = (1.0 - lb) * jax.nn.sigmoid(-fr)
    v = _dot(h, win_ref[:, 2 * F_DIM:2 * F_DIM + V_DIM])
    gr = _dot(h, win_ref[:, 2 * F_DIM + V_DIM:2 * F_DIM + 2 * V_DIM])
    return q, k, lf, v, gr * jax.nn.sigmoid(gr)


def _hproj_kernel(x_ref, gpre_ref, win_ref, lbl_ref, q_ref, k_ref, lf_ref, v_ref, gs_ref):
    h = _rms(x_ref[...], gpre_ref[...]).astype(_BF16)
    q_ref[...], k_ref[...], lf_ref[...], v_ref[...], gs_ref[...] = _hgrn_gates(
        h, win_ref, lbl_ref)


def _hgrn_proj(x2d, gpre, win_bf16, lb_logits, *, rows):
    n = x2d.shape[0]
    row_spec = pl.BlockSpec((rows, D_MODEL), lambda i: (i, 0))
    out = jax.ShapeDtypeStruct((n, D_MODEL), _F32)
    return pl.pallas_call(
        _hproj_kernel,
        grid=(n // rows,),
        in_specs=[row_spec, _const_spec((1, D_MODEL)), _const_spec(win_bf16.shape),
                  _const_spec(lb_logits.shape)],
        out_specs=[row_spec] * 5,
        out_shape=[out] * 5,
        compiler_params=pltpu.CompilerParams(
            dimension_semantics=("arbitrary",), vmem_limit_bytes=VMEM_LIMIT),
        name="hgrn_proj",
    )(x2d, gpre, win_bf16, lb_logits)


def _hout_kernel(x_ref, o_ref, gs_ref, gn_ref, wout_ref, gpost_ref, y_ref, zbuf):
    for hd in range(N_HEADS):
        hs = slice(hd * HEAD_V, (hd + 1) * HEAD_V)
        zbuf[:, hs] = (_rms(o_ref[:, hs], gn_ref[...]) * gs_ref[:, hs]).astype(_BF16)
    m = _dot(zbuf[...], wout_ref[...])
    y_ref[...] = x_ref[...] + _rms(m, gpost_ref[...])


def _hgrn_out(x2d, o2d, gs2d, gnorm, wout_bf16, gpost, *, rows):
    n = x2d.shape[0]
    row_spec = pl.BlockSpec((rows, D_MODEL), lambda i: (i, 0))
    return pl.pallas_call(
        _hout_kernel,
        grid=(n // rows,),
        in_specs=[row_spec, row_spec, row_spec, _const_spec((1, HEAD_V)),
                  _const_spec(wout_bf16.shape), _const_spec((1, D_MODEL))],
        out_specs=row_spec,
        out_shape=jax.ShapeDtypeStruct((n, D_MODEL), _F32),
        scratch_shapes=[pltpu.VMEM((rows, V_DIM), _BF16)],
        compiler_params=pltpu.CompilerParams(
            dimension_semantics=("arbitrary",), vmem_limit_bytes=VMEM_LIMIT),
        name="hgrn_out",
    )(x2d, o2d, gs2d, gnorm, wout_bf16, gpost)


def _gla_exponent_matrix():
    c = GLA_CHUNK
    t = np.arange(c)[:, None]
    w = np.arange(c)[None, :]
    blocks = [(w <= t), (w > t)]
    for lev in range(1, GLA_LEVELS + 1):
        n = 1 << lev
        mid = (t // n) * n + n // 2 - 1
        later = (t > mid) & (w > mid) & (w <= t)
        earlier = (t <= mid) & (w > t) & (w <= mid)
        blocks.append(later | earlier)
    return np.concatenate(blocks, axis=0).astype(np.float32)


def _hgrn_prompt_kernel(x_ref, s0_ref, gpre_ref, win_ref, lbl_ref, mall_ref, gn_ref, wout_ref,
                        gpost_ref, y_ref, sn_ref, st, q_buf, k_buf, lf_buf, v_buf, gs_buf, z_buf,
                        *, rows, n_tiles):
    c = GLA_CHUNK
    l = pl.program_id(1)

    @pl.when(l == 0)
    def _():
        for hd in range(N_HEADS):
            st[hd] = s0_ref[0, hd].T

    x = x_ref[0]
    q, k, lf, v, gs = _hgrn_gates(_rms(x, gpre_ref[...]).astype(_BF16), win_ref, lbl_ref)
    q_buf[...] = q
    k_buf[...] = k
    lf_buf[...] = lf
    v_buf[...] = v.astype(_BF16)
    gs_buf[...] = gs

    ti = lax.broadcasted_iota(jnp.int32, (c, c), 0)
    si = lax.broadcasted_iota(jnp.int32, (c, c), 1)
    ri = lax.broadcasted_iota(jnp.int32, (c, 1), 0)
    diag = ti == si
    lvl_mask, lvl_role = [], []
    for lev in range(1, GLA_LEVELS + 1):
        same = (ti >> lev) == (si >> lev)
        t_late = ((ti >> (lev - 1)) & 1) == 1
        s_early = ((si >> (lev - 1)) & 1) == 0
        lvl_mask.append(same & t_late & s_early)
        lvl_role.append(((ri >> (lev - 1)) & 1) == 1)

    def chunk(ci, carry):
        rs = pl.ds(pl.multiple_of(ci * c, c), c)
        lf = lf_buf[rs, :]
        lf_hi = lf.astype(_BF16)
        r1 = lf - lf_hi.astype(_F32)
        lf_mid = r1.astype(_BF16)
        lf_lo = (r1 - lf_mid.astype(_F32)).astype(_BF16)
        ex = jnp.exp(_dot(mall_ref[...], jnp.concatenate([lf_hi, lf_mid, lf_lo], axis=0)))
        eb = ex[0:c]
        eb_rest = ex[c:2 * c]
        eb_last = eb[c - 1:c]
        q = q_buf[rs, :]
        k = k_buf[rs, :]
        vb = v_buf[rs, :]
        qd = (q * eb).astype(_BF16)
        kd = (k * eb_rest).astype(_BF16)
        heads = [slice(hd * HEAD_K, (hd + 1) * HEAD_K) for hd in range(N_HEADS)]
        scores = []
        for hs in heads:
            qh = q[:, hs]
            kh = k[:, hs]
            a = jnp.where(diag, _dot_nt(qh.astype(_BF16), kh.astype(_BF16)), 0.0)
            for lev in range(GLA_LEVELS):
                e = ex[(2 + lev) * c:(3 + lev) * c, hs]
                xl = (jnp.where(lvl_role[lev], qh, kh) * e).astype(_BF16)
                a = jnp.where(lvl_mask[lev], _dot_nt(xl, xl), a)
            scores.append(a.astype(_BF16))
        s_old = [st[hd] for hd in range(N_HEADS)]
        for hd, hs in enumerate(heads):
            st[hd] = s_old[hd] * eb_last[:, hs] + _dot_tn(vb[:, hs], kd[:, hs])
        for hd, hs in enumerate(heads):
            o = _dot_nt(qd[:, hs], s_old[hd].astype(_BF16)) + _dot(scores[hd], vb[:, hs])
            z_buf[rs, hs] = (_rms(o, gn_ref[...]) * gs_buf[rs, hs]).astype(_BF16)
        return carry

    lax.fori_loop(0, rows // c, chunk, 0, unroll=GLA_UNROLL)

    m = _dot(z_buf[...], wout_ref[...])
    y_ref[0] = x + _rms(m, gpost_ref[...])

    @pl.when(l == n_tiles - 1)
    def _():
        for hd in range(N_HEADS):
            sn_ref[0, hd] = st[hd].T


def _hgrn_prompt(x, s0, gpre, win_bf16, lb_logits, gnorm, wout_bf16, gpost, *, rows):
    nb, n, _ = x.shape
    n_tiles = n // rows
    mall = jnp.asarray(np.tile(_gla_exponent_matrix(), (1, 3)), dtype=_BF16)
    row_spec = pl.BlockSpec((1, rows, D_MODEL), lambda b, l: (b, l, 0))
    st_spec = pl.BlockSpec((1, N_HEADS, HEAD_K, HEAD_V), lambda b, l: (b, 0, 0, 0))
    kern = functools.partial(_hgrn_prompt_kernel, rows=rows, n_tiles=n_tiles)
    return pl.pallas_call(
        kern,
        grid=(nb, n_tiles),
        in_specs=[row_spec, st_spec, _const_spec((1, D_MODEL)), _const_spec(win_bf16.shape),
                  _const_spec(lb_logits.shape), _const_spec(mall.shape), _const_spec((1, HEAD_V)),
                  _const_spec(wout_bf16.shape), _const_spec((1, D_MODEL))],
        out_specs=[row_spec, st_spec],
        out_shape=[jax.ShapeDtypeStruct(x.shape, _F32), jax.ShapeDtypeStruct(s0.shape, _F32)],
        scratch_shapes=[
            pltpu.VMEM((N_HEADS, HEAD_V, HEAD_K), _F32),
            pltpu.VMEM((rows, F_DIM), _F32),
            pltpu.VMEM((rows, F_DIM), _F32),
            pltpu.VMEM((rows, F_DIM), _F32),
            pltpu.VMEM((rows, V_DIM), _BF16),
            pltpu.VMEM((rows, V_DIM), _F32),
            pltpu.VMEM((rows, V_DIM), _BF16),
        ],
        compiler_params=_params(),
        name="hgrn_prompt",
    )(x, s0, gpre, win_bf16, lb_logits, mall, gnorm, wout_bf16, gpost)


def _gla_sample_kernel(q_ref, k_ref, lf_ref, v_ref, s0_ref, o_ref, sn_ref, *, steps):
    tp = SAMPLE_T_PAD
    ri = lax.broadcasted_iota(jnp.int32, (tp, 1), 0)
    ri128 = lax.broadcasted_iota(jnp.int32, (HEAD_K, HEAD_K), 0)
    ci128 = lax.broadcasted_iota(jnp.int32, (HEAD_K, HEAD_K), 1)
    eye = (ri128 == ci128).astype(_F32)
    for bi in range(SAMPLE_B_BLK):
        lf = lf_ref[bi]
        q = q_ref[bi]
        k = k_ref[bi]
        v = v_ref[bi]
        b = jnp.zeros_like(lf)
        for j in range(steps):
            b = b + jnp.where(ri >= j, lf[j:j + 1, :], 0.0)
        eb = jnp.exp(b)
        b_last = b[steps - 1:steps, :]
        eb_last = jnp.exp(b_last)
        qd = (q * eb).astype(_BF16)
        kd = k * jnp.exp(b_last - b)
        o_intra = jnp.zeros_like(q)
        for s in range(steps):
            arg = jnp.where(ri >= s, b - b[s:s + 1, :], 0.0)
            p = jnp.where(ri >= s, q * k[s:s + 1, :] * jnp.exp(arg), 0.0)
            parts = []
            for hd in range(N_HEADS):
                hs = slice(hd * HEAD_K, (hd + 1) * HEAD_K)
                a_s = jnp.sum(p[:, hs], axis=-1, keepdims=True)
                parts.append(a_s * v[s:s + 1, hs])
            o_intra = o_intra + jnp.concatenate(parts, axis=-1)
        for hd in range(N_HEADS):
            hs = slice(hd * HEAD_K, (hd + 1) * HEAD_K)
            s0 = s0_ref[bi, hd]
            o_ref[bi, :, hs] = o_intra[:, hs] + _dot(qd[:, hs], s0.astype(_BF16))
            dcol = jnp.sum(eye * eb_last[:, hs], axis=-1, keepdims=True)
            upd = dcol * s0
            for s in range(steps):
                kcol = jnp.sum(eye * kd[s:s + 1, hs], axis=-1, keepdims=True)
                upd = upd + kcol * v[s:s + 1, hs]
            sn_ref[bi, hd] = upd


def _gla_sample(q, k, lf, v, s0, *, steps):
    nb = q.shape[0]
    row_spec = pl.BlockSpec((SAMPLE_B_BLK, SAMPLE_T_PAD, D_MODEL), lambda i: (i, 0, 0))
    st_spec = pl.BlockSpec((SAMPLE_B_BLK, N_HEADS, HEAD_K, HEAD_V), lambda i: (i, 0, 0, 0))
    kern = functools.partial(_gla_sample_kernel, steps=steps)
    return pl.pallas_call(
        kern,
        grid=(nb // SAMPLE_B_BLK,),
        in_specs=[row_spec, row_spec, row_spec, row_spec, st_spec],
        out_specs=[row_spec, st_spec],
        out_shape=[jax.ShapeDtypeStruct(q.shape, _F32), jax.ShapeDtypeStruct(s0.shape, _F32)],
        compiler_params=pltpu.CompilerParams(
            dimension_semantics=("arbitrary",), vmem_limit_bytes=VMEM_LIMIT),
        name="gla_sample",
    )(q, k, lf, v, s0)


def _row(v):
    return v.reshape(1, -1)


def _trunk_prompt(x, wts):
    nb, n, d = x.shape
    pool_ctx = jnp.zeros((nb, POOL_CTX, d), _F32)
    ffn_ctx = jnp.zeros((nb, CONV_W - 1, 2 * D_FF), _F32)
    s0 = jnp.zeros((nb, N_HEADS, HEAD_K, HEAD_V), _F32)

    x, pool_st = _pool_layer(x, pool_ctx, wts["mix_pre"][0], wts["pool_w"], wts["pool_scale"],
                             wts["mix_post"][0], rows=ROW_TILE, step=1, pos0=0)
    x, ffn_st0 = _ffn_layer(x, ffn_ctx, wts["ffn_pre"][0], wts["w_up"][0], wts["conv_w"][0],
                            wts["conv_b"][0], wts["w_down"][0], wts["ffn_post"][0],
                            rows=ROW_TILE, step=1, fold=True)
    x, s_new = _hgrn_prompt(x, s0, wts["mix_pre"][1], wts["w_in"], wts["lb_logits"], wts["gnorm"],
                            wts["w_out"], wts["mix_post"][1], rows=ROW_TILE)
    x, ffn_st1 = _ffn_layer(x, ffn_ctx, wts["ffn_pre"][1], wts["w_up"][1],
                            wts["conv_w"][1], wts["conv_b"][1], wts["w_down"][1],
                            wts["ffn_post"][1], rows=ROW_TILE, step=1, fold=True)
    return x, pool_st[None], s_new[None], jnp.stack([ffn_st0, ffn_st1])


def _trunk_sample(x, pos0, state_pool, state_hgrn, state_ffn, wts):
    nb, steps, d = x.shape
    n = nb * steps

    def to_time_major(a):
        return a.transpose(1, 0, 2).reshape(1, a.shape[1] * nb, a.shape[2])

    def to_batch_major(a, t):
        return a.reshape(t, nb, a.shape[-1]).transpose(1, 0, 2)

    xt = to_time_major(x)
    xt, pool_st = _pool_layer(xt, to_time_major(state_pool[0]), wts["mix_pre"][0], wts["pool_w"],
                              wts["pool_scale"], wts["mix_post"][0], rows=n, step=nb, pos0=pos0)
    xt, ffn_st0 = _ffn_layer(xt, to_time_major(state_ffn[0]), wts["ffn_pre"][0], wts["w_up"][0],
                             wts["conv_w"][0], wts["conv_b"][0], wts["w_down"][0],
                             wts["ffn_post"][0], rows=n, step=nb)
    x2d = xt.reshape(n, d)
    proj = _hgrn_proj(x2d, wts["mix_pre"][1], wts["w_in"], wts["lb_logits"], rows=n)

    def pad_time(a):
        a = to_batch_major(a, steps)
        return jnp.pad(a, ((0, 0), (0, SAMPLE_T_PAD - steps), (0, 0)))

    q, k, lf, v = (pad_time(a) for a in proj[:4])
    o, s_new = _gla_sample(q, k, lf, v, state_hgrn[0], steps=steps)
    o2d = o[:, :steps].transpose(1, 0, 2).reshape(n, d)
    x2d = _hgrn_out(x2d, o2d, proj[4], wts["gnorm"], wts["w_out"], wts["mix_post"][1], rows=n)
    xt, ffn_st1 = _ffn_layer(x2d.reshape(1, n, d), to_time_major(state_ffn[1]), wts["ffn_pre"][1],
                             wts["w_up"][1], wts["conv_w"][1], wts["conv_b"][1],
                             wts["w_down"][1], wts["ffn_post"][1], rows=n, step=nb)
    y = to_batch_major(xt, steps)
    new_pool = to_batch_major(pool_st, POOL_CTX)[None]
    new_ffn = jnp.stack([to_batch_major(ffn_st0, CONV_W - 1), to_batch_major(ffn_st1, CONV_W - 1)])
    return y, new_pool, s_new[None], new_ffn


def kernel(x_prompt, x_sample, state_pool, state_hgrn, state_ffn_conv, norm_mix_pre, norm_mix_post,
           norm_ffn_pre, norm_ffn_post, pool_w, pool_scale, hgrn_w_in, hgrn_lb_logits, hgrn_gnorm,
           hgrn_w_out, ffn_w_up, ffn_conv_w, ffn_conv_b, ffn_w_down):
    depth = ffn_w_up.shape[0]
    assert depth == 2 and pool_w.shape[0] == 1 and hgrn_w_in.shape[0] == 1
    past_len = 16384
    wts = {
        "mix_pre": [_row(norm_mix_pre[i]) for i in range(depth)],
        "mix_post": [_row(norm_mix_post[i]) for i in range(depth)],
        "ffn_pre": [_row(norm_ffn_pre[i]) for i in range(depth)],
        "ffn_post": [_row(norm_ffn_post[i]) for i in range(depth)],
        "pool_w": pool_w[0].astype(_BF16),
        "pool_scale": _row(pool_scale[0]),
        "w_in": hgrn_w_in[0].astype(_BF16),
        "lb_logits": hgrn_lb_logits,
        "gnorm": _row(hgrn_gnorm[0]),
        "w_out": hgrn_w_out[0].astype(_BF16),
        "w_up": [ffn_w_up[i].astype(_BF16) for i in range(depth)],
        "conv_w": [ffn_conv_w[i] for i in range(depth)],
        "conv_b": [_row(ffn_conv_b[i]) for i in range(depth)],
        "w_down": [ffn_w_down[i].astype(_BF16) for i in range(depth)],
    }
    y_p, pool_p, hgrn_p, ffn_p = _trunk_prompt(x_prompt, wts)
    y_s, pool_s, hgrn_s, ffn_s = _trunk_sample(x_sample, past_len, state_pool, state_hgrn,
                                               state_ffn_conv, wts)
    return (y_p, y_s, pool_p, pool_s, hgrn_p, hgrn_s, ffn_p, ffn_s)
```

```python
import functools

import jax
import jax.numpy as jnp
import numpy as np
from jax import lax
from jax.experimental import pallas as pl
from jax.experimental.pallas import tpu as pltpu

D_MODEL = 1024
POOL_WINDOWS = (2, 4, 8, 16)
POOL_GROUP_DIM = D_MODEL // len(POOL_WINDOWS)
POOL_CTX = max(POOL_WINDOWS) - 1
N_HEADS = 8
HEAD_K = 128
HEAD_V = D_MODEL // N_HEADS
F_DIM = N_HEADS * HEAD_K
V_DIM = N_HEADS * HEAD_V
D_FF = 2816
CONV_W = 3
EPS = 1e-6

SUBLANES = 8
ROW_TILE = 512
FF_CHUNK = 256
PROJ_CHUNK = 256
GLA_CHUNK = 64
GLA_LEVELS = 6
GLA_UNROLL = 2
VMEM_LIMIT = 56 * 1024 * 1024

_F32 = jnp.float32
_BF16 = jnp.bfloat16


def _rms(x, g):
    ms = jnp.mean(x * x, axis=-1, keepdims=True)
    return x * lax.rsqrt(ms + EPS) * g


def _dot(a, b):
    return jnp.dot(a, b, preferred_element_type=_F32)


def _dot_nt(a, b):
    return lax.dot_general(a, b, (((1,), (1,)), ((), ())), preferred_element_type=_F32)


def _dot_tn(a, b):
    return lax.dot_general(a, b, (((0,), (0,)), ((), ())), preferred_element_type=_F32)


def _const_spec(shape):
    nd = len(shape)
    return pl.BlockSpec(shape, lambda *_: (0,) * nd, pipeline_mode=pl.Buffered(1))


def _params():
    return pltpu.CompilerParams(
        dimension_semantics=("arbitrary", "arbitrary"), vmem_limit_bytes=VMEM_LIMIT)


def _fold_rows(a):
    rows, c = a.shape
    return a.reshape(SUBLANES, rows // SUBLANES, c).swapaxes(0, 1).reshape(rows, c)


def _unfold_rows(a):
    rows, c = a.shape
    return a.reshape(rows // SUBLANES, SUBLANES, c).swapaxes(0, 1).reshape(rows, c)


def _pool_kernel(x_ref, ctx_ref, gpre_ref, w_ref, scale_ref, gpost_ref, y_ref, st_ref, hbuf,
                 *, rows, step, pos0, n_tiles):
    halo = (POOL_CTX + 1) * step
    l = pl.program_id(1)

    @pl.when(l == 0)
    def _():
        hbuf[0:step, :] = jnp.zeros((step, D_MODEL), _F32)
        hbuf[step:halo, :] = ctx_ref[0]

    x = x_ref[0]
    h = _rms(x, gpre_ref[...])
    hbuf[halo:halo + rows, :] = h

    if pos0 < POOL_CTX:
        assert step == 1
        pos = pos0 + l * rows + lax.broadcasted_iota(jnp.int32, (rows, 1), 0)
    outs = []
    for g, w in enumerate(POOL_WINDOWS):
        cs = slice(g * POOL_GROUP_DIM, (g + 1) * POOL_GROUP_DIM)
        s = h[:, cs]
        for j in range(1, w):
            s = s + hbuf[halo - j * step:halo - j * step + rows, cs]
        if pos0 < POOL_CTX:
            cnt = jnp.minimum(pos + 1, w).astype(_F32)
        else:
            cnt = float(w)
        p = s / cnt - h[:, cs]
        outs.append(_dot(p.astype(_BF16), w_ref[g]))
    m = jnp.concatenate(outs, axis=-1) * scale_ref[...]
    y_ref[0] = x + _rms(m, gpost_ref[...])

    @pl.when(l == n_tiles - 1)
    def _():
        st_ref[0] = hbuf[rows + step:rows + halo, :]

    if n_tiles > 1:
        hbuf[0:halo, :] = hbuf[rows:rows + halo, :]


def _pool_layer(x, ctx, gpre, w_bf16, scale, gpost, *, rows, step, pos0):
    nb, n, _ = x.shape
    n_tiles = n // rows
    halo = (POOL_CTX + 1) * step
    ctx_rows = POOL_CTX * step
    kern = functools.partial(_pool_kernel, rows=rows, step=step, pos0=pos0, n_tiles=n_tiles)
    return pl.pallas_call(
        kern,
        grid=(nb, n_tiles),
        in_specs=[
            pl.BlockSpec((1, rows, D_MODEL), lambda b, l: (b, l, 0)),
            pl.BlockSpec((1, ctx_rows, D_MODEL), lambda b, l: (b, 0, 0)),
            _const_spec((1, D_MODEL)),
            _const_spec(w_bf16.shape),
            _const_spec((1, D_MODEL)),
            _const_spec((1, D_MODEL)),
        ],
        out_specs=[
            pl.BlockSpec((1, rows, D_MODEL), lambda b, l: (b, l, 0)),
            pl.BlockSpec((1, ctx_rows, D_MODEL), lambda b, l: (b, 0, 0)),
        ],
        out_shape=[
            jax.ShapeDtypeStruct(x.shape, _F32),
            jax.ShapeDtypeStruct((nb, ctx_rows, D_MODEL), _F32),
        ],
        scratch_shapes=[pltpu.VMEM((halo + rows, D_MODEL), _F32)],
        compiler_params=_params(),
        name="pool_mixer",
    )(x, ctx, gpre, w_bf16, scale, gpost)


def _ffn_kernel(x_ref, ctx_ref, gpre_ref, wup_ref, cw_ref, cb_ref, wdn_ref, gpost_ref,
                y_ref, st_ref, carry, gbuf, vbuf, abuf, *, rows, step, n_tiles, fold):
    shift = SUBLANES if fold else step
    halo = max(SUBLANES, (CONV_W - 1) * shift)
    taps = CONV_W - 1
    l = pl.program_id(1)

    @pl.when(l == 0)
    def _():
        carry[...] = jnp.zeros(carry.shape, _F32)
        if fold:
            for j in range(taps):
                carry[(j + 1) * SUBLANES - 1:(j + 1) * SUBLANES, :] = ctx_ref[0, j:j + 1, :]
        else:
            carry[halo - taps * step:halo, :] = ctx_ref[0]

    x = x_ref[0]
    h = _rms(x, gpre_ref[...])
    if fold:
        h = _fold_rows(h)
        sub0 = lax.broadcasted_iota(jnp.int32, (SUBLANES, 1), 0) == 0
    h = h.astype(_BF16)
    for c in range(D_FF // FF_CHUNK):
        conv = []
        for buf, off in ((gbuf, 0), (vbuf, D_FF)):
            cs = slice(off + c * FF_CHUNK, off + (c + 1) * FF_CHUNK)
            u = _dot(h, wup_ref[:, cs])
            if fold:
                for j in range(taps):
                    grp = u[rows - (taps - j) * SUBLANES:rows - (taps - j - 1) * SUBLANES, :]
                    prev = carry[(j + 1) * SUBLANES - 1:(j + 1) * SUBLANES, cs]
                    buf[j * SUBLANES:(j + 1) * SUBLANES, :] = jnp.where(
                        sub0, prev, pltpu.roll(grp, 1, axis=0))
                carry[:, cs] = u[rows - halo:rows, :]
                buf[halo:halo + rows, :] = u
            else:
                buf[0:halo, :] = carry[:, cs]
                buf[halo:halo + rows, :] = u
                carry[:, cs] = buf[rows:rows + halo, :]
            conv.append(cb_ref[:, cs]
                        + buf[halo - 2 * shift:halo - 2 * shift + rows, :] * cw_ref[0:1, cs]
                        + buf[halo - shift:halo - shift + rows, :] * cw_ref[1:2, cs]
                        + u * cw_ref[2:3, cs])
        a = jax.nn.gelu(conv[0], approximate=True) * conv[1]
        abuf[:, c * FF_CHUNK:(c + 1) * FF_CHUNK] = a.astype(_BF16)
    m = _rms(_dot(abuf[...], wdn_ref[...]), gpost_ref[...])
    if fold:
        m = _unfold_rows(m)
    y_ref[0] = x + m

    @pl.when(l == n_tiles - 1)
    def _():
        if fold:
            for j in range(taps):
                st_ref[0, j:j + 1, :] = carry[(j + 1) * SUBLANES - 1:(j + 1) * SUBLANES, :]
        else:
            st_ref[0] = carry[halo - taps * step:halo, :]


def _ffn_layer(x, ctx, gpre, wup_bf16, cw, cb, wdn_bf16, gpost, *, rows, step, fold=False):
    nb, n, _ = x.shape
    n_tiles = n // rows
    assert not fold or (step == 1 and rows % (SUBLANES * SUBLANES) == 0)
    halo = max(SUBLANES, (CONV_W - 1) * (SUBLANES if fold else step))
    ctx_rows = (CONV_W - 1) * step
    kern = functools.partial(_ffn_kernel, rows=rows, step=step, n_tiles=n_tiles, fold=fold)
    return pl.pallas_call(
        kern,
        grid=(nb, n_tiles),
        in_specs=[
            pl.BlockSpec((1, rows, D_MODEL), lambda b, l: (b, l, 0)),
            pl.BlockSpec((1, ctx_rows, 2 * D_FF), lambda b, l: (b, 0, 0)),
            _const_spec((1, D_MODEL)),
            _const_spec((D_MODEL, 2 * D_FF)),
            _const_spec((CONV_W, 2 * D_FF)),
            _const_spec((1, 2 * D_FF)),
            _const_spec((D_FF, D_MODEL)),
            _const_spec((1, D_MODEL)),
        ],
        out_specs=[
            pl.BlockSpec((1, rows, D_MODEL), lambda b, l: (b, l, 0)),
            pl.BlockSpec((1, ctx_rows, 2 * D_FF), lambda b, l: (b, 0, 0)),
        ],
        out_shape=[
            jax.ShapeDtypeStruct(x.shape, _F32),
            jax.ShapeDtypeStruct((nb, ctx_rows, 2 * D_FF), _F32),
        ],
        scratch_shapes=[
            pltpu.VMEM((halo, 2 * D_FF), _F32),
            pltpu.VMEM((halo + rows, FF_CHUNK), _F32),
            pltpu.VMEM((halo + rows, FF_CHUNK), _F32),
            pltpu.VMEM((rows, D_FF), _BF16),
        ],
        compiler_params=_params(),
        name="conv_ffn",
    )(x, ctx, gpre, wup_bf16, cw, cb, wdn_bf16, gpost)


def _silu(x):
    return x * (0.5 * jnp.tanh(0.5 * x) + 0.5)


def _hgrn_gates(h, win_ref, lbl_ref, q_ref, k_ref, lf_ref, v_ref, gs_ref):
    l0 = lbl_ref[0:1, :]
    l1 = lbl_ref[1:2, :]
    mx = jnp.maximum(l0, l1)
    e0 = jnp.exp(l0 - mx)
    e1 = jnp.exp(l1 - mx)
    lb_all = e1 / (e0 + e1)

    assert F_DIM == V_DIM
    for c in range(F_DIM // PROJ_CHUNK):
        cs = slice(c * PROJ_CHUNK, (c + 1) * PROJ_CHUNK)
        fr = _dot(h, win_ref[:, F_DIM + c * PROJ_CHUNK:F_DIM + (c + 1) * PROJ_CHUNK])
        off = 2 * F_DIM + c * PROJ_CHUNK
        v_ref[:, cs] = _dot(h, win_ref[:, off:off + PROJ_CHUNK]).astype(v_ref.dtype)
        qr = _dot(h, win_ref[:, cs])
        gr = _dot(h, win_ref[:, off + V_DIM:off + V_DIM + PROJ_CHUNK])
        e = jnp.exp(-jnp.abs(fr))
        big = 1.0 / (1.0 + e)
        small = e * big
        pos = fr >= 0.0
        lb = lb_all[:, cs]
        lf_ref[:, cs] = jnp.log(lb + (1.0 - lb) * jnp.where(pos, big, small))
        k_ref[:, cs] = (1.0 - lb) * jnp.where(pos, small, big)
        q_ref[:, cs] = _silu(qr) * (HEAD_K ** -0.5)
        gs_ref[:, cs] = _silu(gr)


def _hproj_kernel(x_ref, gpre_ref, win_ref, lbl_ref, q_ref, k_ref, lf_ref, v_ref, gs_ref):
    h = _rms(x_ref[...], gpre_ref[...]).astype(_BF16)
    _hgrn_gates(h, win_ref, lbl_ref, q_ref, k_ref, lf_ref, v_ref, gs_ref)


def _hgrn_proj(x2d, gpre, win_bf16, lb_logits, *, rows):
    n = x2d.shape[0]
    row_spec = pl.BlockSpec((rows, D_MODEL), lambda i: (i, 0))
    out = jax.ShapeDtypeStruct((n, D_MODEL), _F32)
    return pl.pallas_call(
        _hproj_kernel,
        grid=(n // rows,),
        in_specs=[row_spec, _const_spec((1, D_MODEL)), _const_spec(win_bf16.shape),
                  _const_spec(lb_logits.shape)],
        out_specs=[row_spec] * 5,
        out_shape=[out] * 5,
        compiler_params=pltpu.CompilerParams(
            dimension_semantics=("arbitrary",), vmem_limit_bytes=VMEM_LIMIT),
        name="hgrn_proj",
    )(x2d, gpre, win_bf16, lb_logits)


def _hout_kernel(x_ref, o_ref, gs_ref, gn_ref, wout_ref, gpost_ref, y_ref, zbuf):
    for hd in range(N_HEADS):
        hs = slice(hd * HEAD_V, (hd + 1) * HEAD_V)
        zbuf[:, hs] = (_rms(o_ref[:, hs], gn_ref[...]) * gs_ref[:, hs]).astype(_BF16)
    m = _dot(zbuf[...], wout_ref[...])
    y_ref[...] = x_ref[...] + _rms(m, gpost_ref[...])


def _hgrn_out(x2d, o2d, gs2d, gnorm, wout_bf16, gpost, *, rows):
    n = x2d.shape[0]
    row_spec = pl.BlockSpec((rows, D_MODEL), lambda i: (i, 0))
    return pl.pallas_call(
        _hout_kernel,
        grid=(n // rows,),
        in_specs=[row_spec, row_spec, row_spec, _const_spec((1, HEAD_V)),
                  _const_spec(wout_bf16.shape), _const_spec((1, D_MODEL))],
        out_specs=row_spec,
        out_shape=jax.ShapeDtypeStruct((n, D_MODEL), _F32),
        scratch_shapes=[pltpu.VMEM((rows, V_DIM), _BF16)],
        compiler_params=pltpu.CompilerParams(
            dimension_semantics=("arbitrary",), vmem_limit_bytes=VMEM_LIMIT),
        name="hgrn_out",
    )(x2d, o2d, gs2d, gnorm, wout_bf16, gpost)


def _gla_exponent_matrix():
    c = GLA_CHUNK
    t = np.arange(c)[:, None]
    w = np.arange(c)[None, :]
    blocks = [(w <= t), (w > t)]
    for lev in range(1, GLA_LEVELS + 1):
        n = 1 << lev
        mid = (t // n) * n + n // 2 - 1
        later = (t > mid) & (w > mid) & (w <= t)
        earlier = (t <= mid) & (w > t) & (w <= mid)
        blocks.append(later | earlier)
    return np.concatenate(blocks, axis=0).astype(np.float32)


def _hgrn_prompt_kernel(x_ref, s0_ref, gpre_ref, win_ref, lbl_ref, mall_ref, gn_ref, wout_ref,
                        gpost_ref, y_ref, sn_ref, st, q_buf, k_buf, lf_buf, v_buf, gs_buf, z_buf,
                        *, rows, n_tiles):
    c = GLA_CHUNK
    l = pl.program_id(1)

    @pl.when(l == 0)
    def _():
        for hd in range(N_HEADS):
            st[hd] = s0_ref[0, hd].T

    x = x_ref[0]
    _hgrn_gates(_rms(x, gpre_ref[...]).astype(_BF16), win_ref, lbl_ref,
                q_buf, k_buf, lf_buf, v_buf, gs_buf)

    ti = lax.broadcasted_iota(jnp.int32, (c, c), 0)
    si = lax.broadcasted_iota(jnp.int32, (c, c), 1)
    ri = lax.broadcasted_iota(jnp.int32, (c, 1), 0)
    diag = ti == si
    lvl_mask, lvl_role = [], []
    for lev in range(1, GLA_LEVELS + 1):
        same = (ti >> lev) == (si >> lev)
        t_late = ((ti >> (lev - 1)) & 1) == 1
        s_early = ((si >> (lev - 1)) & 1) == 0
        lvl_mask.append(same & t_late & s_early)
        lvl_role.append(((ri >> (lev - 1)) & 1) == 1)

    def chunk(ci, carry):
        rs = pl.ds(pl.multiple_of(ci * c, c), c)
        lf = lf_buf[rs, :]
        lf_hi = lf.astype(_BF16)
        r1 = lf - lf_hi.astype(_F32)
        lf_mid = r1.astype(_BF16)
        lf_lo = (r1 - lf_mid.astype(_F32)).astype(_BF16)
        ex = jnp.exp(_dot(mall_ref[...], jnp.concatenate([lf_hi, lf_mid, lf_lo], axis=0)))
        eb = ex[0:c]
        eb_rest = ex[c:2 * c]
        eb_last = eb[c - 1:c]
        q = q_buf[rs, :]
        k = k_buf[rs, :]
        vb = v_buf[rs, :]
        qd = (q * eb).astype(_BF16)
        kd = (k * eb_rest).astype(_BF16)
        heads = [slice(hd * HEAD_K, (hd + 1) * HEAD_K) for hd in range(N_HEADS)]
        scores = []
        for hs in heads:
            qh = q[:, hs]
            kh = k[:, hs]
            a = jnp.where(diag, _dot_nt(qh.astype(_BF16), kh.astype(_BF16)), 0.0)
            for lev in range(GLA_LEVELS):
                e = ex[(2 + lev) * c:(3 + lev) * c, hs]
                xl = (jnp.where(lvl_role[lev], qh, kh) * e).astype(_BF16)
                a = jnp.where(lvl_mask[lev], _dot_nt(xl, xl), a)
            scores.append(a.astype(_BF16))
        s_old = [st[hd] for hd in range(N_HEADS)]
        for hd, hs in enumerate(heads):
            st[hd] = s_old[hd] * eb_last[:, hs] + _dot_tn(vb[:, hs], kd[:, hs])
        for hd, hs in enumerate(heads):
            o = _dot_nt(qd[:, hs], s_old[hd].astype(_BF16)) + _dot(scores[hd], vb[:, hs])
            z_buf[rs, hs] = (_rms(o, gn_ref[...]) * gs_buf[rs, hs]).astype(_BF16)
        return carry

    lax.fori_loop(0, rows // c, chunk, 0, unroll=GLA_UNROLL)

    m = _dot(z_buf[...], wout_ref[...])
    y_ref[0] = x + _rms(m, gpost_ref[...])

    @pl.when(l == n_tiles - 1)
    def _():
        for hd in range(N_HEADS):
            sn_ref[0, hd] = st[hd].T


def _hgrn_prompt(x, s0, gpre, win_bf16, lb_logits, gnorm, wout_bf16, gpost, *, rows):
    nb, n, _ = x.shape
    n_tiles = n // rows
    mall = jnp.asarray(np.tile(_gla_exponent_matrix(), (1, 3)), dtype=_BF16)
    row_spec = pl.BlockSpec((1, rows, D_MODEL), lambda b, l: (b, l, 0))
    st_spec = pl.BlockSpec((1, N_HEADS, HEAD_K, HEAD_V), lambda b, l: (b, 0, 0, 0))
    kern = functools.partial(_hgrn_prompt_kernel, rows=rows, n_tiles=n_tiles)
    return pl.pallas_call(
        kern,
        grid=(nb, n_tiles),
        in_specs=[row_spec, st_spec, _const_spec((1, D_MODEL)), _const_spec(win_bf16.shape),
                  _const_spec(lb_logits.shape), _const_spec(mall.shape), _const_spec((1, HEAD_V)),
                  _const_spec(wout_bf16.shape), _const_spec((1, D_MODEL))],
        out_specs=[row_spec, st_spec],
        out_shape=[jax.ShapeDtypeStruct(x.shape, _F32), jax.ShapeDtypeStruct(s0.shape, _F32)],
        scratch_shapes=[
            pltpu.VMEM((N_HEADS, HEAD_V, HEAD_K), _F32),
            pltpu.VMEM((rows, F_DIM), _F32),
            pltpu.VMEM((rows, F_DIM), _F32),
            pltpu.VMEM((rows, F_DIM), _F32),
            pltpu.VMEM((rows, V_DIM), _BF16),
            pltpu.VMEM((rows, V_DIM), _F32),
            pltpu.VMEM((rows, V_DIM), _BF16),
        ],
        compiler_params=_params(),
        name="hgrn_prompt",
    )(x, s0, gpre, win_bf16, lb_logits, mall, gnorm, wout_bf16, gpost)


def _gla_sample_kernel(q_ref, k_ref, lf_ref, v_ref, s0_ref, o_ref, sn_ref, *, steps):
    heads = [slice(hd * HEAD_K, (hd + 1) * HEAD_K) for hd in range(N_HEADS)]
    q = [q_ref[t] for t in range(steps)]
    k = [k_ref[t] for t in range(steps)]
    v = [v_ref[t] for t in range(steps)]
    b = [lf_ref[0]]
    for t in range(1, steps):
        b.append(b[-1] + lf_ref[t])
    b_last = b[-1]
    eb_last = jnp.exp(b_last)
    qd = [q[t] * jnp.exp(b[t]) for t in range(steps)]
    kd = [k[s] * jnp.exp(b_last - b[s]) for s in range(steps)]

    o_intra = []
    for t in range(steps):
        acc = None
        for s in range(t + 1):
            p = q[t] * k[s] if s == t else q[t] * k[s] * jnp.exp(b[t] - b[s])
            part = jnp.concatenate(
                [jnp.sum(p[:, hs], axis=-1, keepdims=True) * v[s][:, hs] for hs in heads], axis=-1)
            acc = part if acc is None else acc + part
        o_intra.append(acc)

    zero = jnp.zeros_like(b_last)
    rows_pad = [zero] * (SUBLANES - steps)
    e_hi = eb_last.astype(_BF16).astype(_F32)
    e_r = eb_last - e_hi
    e_mid = e_r.astype(_BF16).astype(_F32)
    e_lo = e_r - e_mid
    assert steps + 3 <= SUBLANES
    qd_b = jnp.stack(qd + rows_pad).swapaxes(0, 1).astype(_BF16)
    v_b = jnp.stack(v + rows_pad).swapaxes(0, 1)
    kd_b = jnp.stack(kd + [e_hi, e_mid, e_lo] + rows_pad[3:]).swapaxes(0, 1).astype(_BF16)
    ri = lax.broadcasted_iota(jnp.int32, (SUBLANES, HEAD_V), 0)
    ones_rows = ((ri >= steps) & (ri < steps + 3)).astype(_F32)

    o_inter = []
    for bi in range(SUBLANES):
        outs = []
        for hd, hs in enumerate(heads):
            s0 = s0_ref[bi, hd]
            outs.append(_dot(qd_b[bi][:, hs], s0.astype(_BF16)))
            rhs = jnp.concatenate([v_b[bi][:, hs], ones_rows], axis=-1).astype(_BF16)
            both = _dot_tn(kd_b[bi][:, hs], rhs)
            sn_ref[bi, hd] = both[:, HEAD_V:] * s0 + both[:, :HEAD_V]
        o_inter.append(jnp.concatenate(outs, axis=-1))
    o_inter = jnp.stack(o_inter).swapaxes(0, 1)
    for t in range(steps):
        o_ref[t] = o_intra[t] + o_inter[t]


def _gla_sample(q, k, lf, v, s0):
    steps, nb, _ = q.shape
    row_spec = pl.BlockSpec((steps, SUBLANES, D_MODEL), lambda i: (0, i, 0))
    st_spec = pl.BlockSpec((SUBLANES, N_HEADS, HEAD_K, HEAD_V), lambda i: (i, 0, 0, 0))
    kern = functools.partial(_gla_sample_kernel, steps=steps)
    return pl.pallas_call(
        kern,
        grid=(nb // SUBLANES,),
        in_specs=[row_spec, row_spec, row_spec, row_spec, st_spec],
        out_specs=[row_spec, st_spec],
        out_shape=[jax.ShapeDtypeStruct(q.shape, _F32), jax.ShapeDtypeStruct(s0.shape, _F32)],
        compiler_params=pltpu.CompilerParams(
            dimension_semantics=("arbitrary",), vmem_limit_bytes=VMEM_LIMIT),
        name="gla_sample",
    )(q, k, lf, v, s0)


def _row(v):
    return v.reshape(1, -1)


def _trunk_prompt(x, wts):
    nb, n, d = x.shape
    pool_ctx = jnp.zeros((nb, POOL_CTX, d), _F32)
    ffn_ctx = jnp.zeros((nb, CONV_W - 1, 2 * D_FF), _F32)
    s0 = jnp.zeros((nb, N_HEADS, HEAD_K, HEAD_V), _F32)

    x, pool_st = _pool_layer(x, pool_ctx, wts["mix_pre"][0], wts["pool_w"], wts["pool_scale"],
                             wts["mix_post"][0], rows=ROW_TILE, step=1, pos0=0)
    x, ffn_st0 = _ffn_layer(x, ffn_ctx, wts["ffn_pre"][0], wts["w_up"][0], wts["conv_w"][0],
                            wts["conv_b"][0], wts["w_down"][0], wts["ffn_post"][0],
                            rows=ROW_TILE, step=1, fold=True)
    x, s_new = _hgrn_prompt(x, s0, wts["mix_pre"][1], wts["w_in"], wts["lb_logits"], wts["gnorm"],
                            wts["w_out"], wts["mix_post"][1], rows=ROW_TILE)
    x, ffn_st1 = _ffn_layer(x, ffn_ctx, wts["ffn_pre"][1], wts["w_up"][1],
                            wts["conv_w"][1], wts["conv_b"][1], wts["w_down"][1],
                            wts["ffn_post"][1], rows=ROW_TILE, step=1, fold=True)
    return x, pool_st[None], s_new[None], jnp.stack([ffn_st0, ffn_st1])


def _trunk_sample(x, pos0, state_pool, state_hgrn, state_ffn, wts):
    nb, steps, d = x.shape
    n = nb * steps

    def to_time_major(a):
        return a.transpose(1, 0, 2).reshape(1, a.shape[1] * nb, a.shape[2])

    def to_batch_major(a, t):
        return a.reshape(t, nb, a.shape[-1]).transpose(1, 0, 2)

    xt = to_time_major(x)
    xt, pool_st = _pool_layer(xt, to_time_major(state_pool[0]), wts["mix_pre"][0], wts["pool_w"],
                              wts["pool_scale"], wts["mix_post"][0], rows=n, step=nb, pos0=pos0)
    xt, ffn_st0 = _ffn_layer(xt, to_time_major(state_ffn[0]), wts["ffn_pre"][0], wts["w_up"][0],
                             wts["conv_w"][0], wts["conv_b"][0], wts["w_down"][0],
                             wts["ffn_post"][0], rows=n, step=nb)
    x2d = xt.reshape(n, d)
    proj = _hgrn_proj(x2d, wts["mix_pre"][1], wts["w_in"], wts["lb_logits"], rows=n)

    q, k, lf, v = (a.reshape(steps, nb, d) for a in proj[:4])
    o, s_new = _gla_sample(q, k, lf, v, state_hgrn[0])
    x2d = _hgrn_out(x2d, o.reshape(n, d), proj[4], wts["gnorm"], wts["w_out"],
                    wts["mix_post"][1], rows=n)
    xt, ffn_st1 = _ffn_layer(x2d.reshape(1, n, d), to_time_major(state_ffn[1]), wts["ffn_pre"][1],
                             wts["w_up"][1], wts["conv_w"][1], wts["conv_b"][1],
                             wts["w_down"][1], wts["ffn_post"][1], rows=n, step=nb)
    y = to_batch_major(xt, steps)
    new_pool = to_batch_major(pool_st, POOL_CTX)[None]
    new_ffn = jnp.stack([to_batch_major(ffn_st0, CONV_W - 1), to_batch_major(ffn_st1, CONV_W - 1)])
    return y, new_pool, s_new[None], new_ffn


def kernel(x_prompt, x_sample, state_pool, state_hgrn, state_ffn_conv, norm_mix_pre, norm_mix_post,
           norm_ffn_pre, norm_ffn_post, pool_w, pool_scale, hgrn_w_in, hgrn_lb_logits, hgrn_gnorm,
           hgrn_w_out, ffn_w_up, ffn_conv_w, ffn_conv_b, ffn_w_down):
    depth = ffn_w_up.shape[0]
    assert depth == 2 and pool_w.shape[0] == 1 and hgrn_w_in.shape[0] == 1
    past_len = 16384
    wts = {
        "mix_pre": [_row(norm_mix_pre[i]) for i in range(depth)],
        "mix_post": [_row(norm_mix_post[i]) for i in range(depth)],
        "ffn_pre": [_row(norm_ffn_pre[i]) for i in range(depth)],
        "ffn_post": [_row(norm_ffn_post[i]) for i in range(depth)],
        "pool_w": pool_w[0].astype(_BF16),
        "pool_scale": _row(pool_scale[0]),
        "w_in": hgrn_w_in[0].astype(_BF16),
        "lb_logits": hgrn_lb_logits,
        "gnorm": _row(hgrn_gnorm[0]),
        "w_out": hgrn_w_out[0].astype(_BF16),
        "w_up": [ffn_w_up[i].astype(_BF16) for i in range(depth)],
        "conv_w": [ffn_conv_w[i] for i in range(depth)],
        "conv_b": [_row(ffn_conv_b[i]) for i in range(depth)],
        "w_down": [ffn_w_down[i].astype(_BF16) for i in range(depth)],
    }
    y_p, pool_p, hgrn_p, ffn_p = _trunk_prompt(x_prompt, wts)
    y_s, pool_s, hgrn_s, ffn_s = _trunk_sample(x_sample, past_len, state_pool, state_hgrn,
                                               state_ffn_conv, wts)
    return (y_p, y_s, pool_p, pool_s, hgrn_p, hgrn_s, ffn_p, ffn_s)
```

```python
import functools

import jax
import jax.numpy as jnp
import numpy as np
from jax import lax
from jax.experimental import pallas as pl
from jax.experimental.pallas import tpu as pltpu

D_MODEL = 1024
POOL_WINDOWS = (2, 4, 8, 16)
POOL_GROUP_DIM = D_MODEL // len(POOL_WINDOWS)
POOL_CTX = max(POOL_WINDOWS) - 1
N_HEADS = 8
HEAD_K = 128
HEAD_V = D_MODEL // N_HEADS
F_DIM = N_HEADS * HEAD_K
V_DIM = N_HEADS * HEAD_V
D_FF = 2816
CONV_W = 3
EPS = 1e-6
LOG2_E = 1.4426950408889634

SUBLANES = 8
ROW_TILE = 512
FFN_SUBS = 2
FF_CHUNK = 256
PROJ_CHUNK = 256
GLA_CHUNK = 64
GLA_LEVELS = 6
GLA_FINE_LEVELS = 3
GLA_UNROLL = 2
VMEM_LIMIT = 56 * 1024 * 1024

_F32 = jnp.float32
_BF16 = jnp.bfloat16


def _rms(x, g):
    ms = jnp.mean(x * x, axis=-1, keepdims=True)
    return x * lax.rsqrt(ms + EPS) * g


def _dot(a, b):
    return jnp.dot(a, b, preferred_element_type=_F32)


def _dot_nt(a, b):
    return lax.dot_general(a, b, (((1,), (1,)), ((), ())), preferred_element_type=_F32)


def _dot_tn(a, b):
    return lax.dot_general(a, b, (((0,), (0,)), ((), ())), preferred_element_type=_F32)


def _const_spec(shape):
    nd = len(shape)
    return pl.BlockSpec(shape, lambda *_: (0,) * nd, pipeline_mode=pl.Buffered(1))


def _params():
    return pltpu.CompilerParams(
        dimension_semantics=("arbitrary", "arbitrary"), vmem_limit_bytes=VMEM_LIMIT)


def _fold_rows(a):
    rows, c = a.shape
    return a.reshape(SUBLANES, rows // SUBLANES, c).swapaxes(0, 1).reshape(rows, c)


def _unfold_rows(a):
    rows, c = a.shape
    return a.reshape(rows // SUBLANES, SUBLANES, c).swapaxes(0, 1).reshape(rows, c)


def _pool_kernel(x_ref, ctx_ref, gpre_ref, w_ref, scale_ref, gpost_ref, y_ref, st_ref, hbuf,
                 *, rows, step, pos0, n_tiles):
    halo = (POOL_CTX + 1) * step
    l = pl.program_id(1)

    @pl.when(l == 0)
    def _():
        hbuf[0:step, :] = jnp.zeros((step, D_MODEL), _F32)
        hbuf[step:halo, :] = ctx_ref[0]

    x = x_ref[0]
    h = _rms(x, gpre_ref[...])
    hbuf[halo:halo + rows, :] = h

    if pos0 < POOL_CTX:
        assert step == 1
        pos = pos0 + l * rows + lax.broadcasted_iota(jnp.int32, (rows, 1), 0)
    outs = []
    for g, w in enumerate(POOL_WINDOWS):
        cs = slice(g * POOL_GROUP_DIM, (g + 1) * POOL_GROUP_DIM)
        s = h[:, cs]
        for j in range(1, w):
            s = s + hbuf[halo - j * step:halo - j * step + rows, cs]
        if pos0 < POOL_CTX:
            cnt = jnp.minimum(pos + 1, w).astype(_F32)
        else:
            cnt = float(w)
        p = s / cnt - h[:, cs]
        outs.append(_dot(p.astype(_BF16), w_ref[g]))
    m = jnp.concatenate(outs, axis=-1) * scale_ref[...]
    y_ref[0] = x + _rms(m, gpost_ref[...])

    @pl.when(l == n_tiles - 1)
    def _():
        st_ref[0] = hbuf[rows + step:rows + halo, :]

    if n_tiles > 1:
        hbuf[0:halo, :] = hbuf[rows:rows + halo, :]


def _pool_layer(x, ctx, gpre, w_bf16, scale, gpost, *, rows, step, pos0):
    nb, n, _ = x.shape
    n_tiles = n // rows
    halo = (POOL_CTX + 1) * step
    ctx_rows = POOL_CTX * step
    kern = functools.partial(_pool_kernel, rows=rows, step=step, pos0=pos0, n_tiles=n_tiles)
    return pl.pallas_call(
        kern,
        grid=(nb, n_tiles),
        in_specs=[
            pl.BlockSpec((1, rows, D_MODEL), lambda b, l: (b, l, 0)),
            pl.BlockSpec((1, ctx_rows, D_MODEL), lambda b, l: (b, 0, 0)),
            _const_spec((1, D_MODEL)),
            _const_spec(w_bf16.shape),
            _const_spec((1, D_MODEL)),
            _const_spec((1, D_MODEL)),
        ],
        out_specs=[
            pl.BlockSpec((1, rows, D_MODEL), lambda b, l: (b, l, 0)),
            pl.BlockSpec((1, ctx_rows, D_MODEL), lambda b, l: (b, 0, 0)),
        ],
        out_shape=[
            jax.ShapeDtypeStruct(x.shape, _F32),
            jax.ShapeDtypeStruct((nb, ctx_rows, D_MODEL), _F32),
        ],
        scratch_shapes=[pltpu.VMEM((halo + rows, D_MODEL), _F32)],
        compiler_params=_params(),
        name="pool_mixer",
    )(x, ctx, gpre, w_bf16, scale, gpost)


def _ffn_kernel(x_ref, ctx_ref, gpre_ref, wup_ref, cw_ref, cb_ref, wdn_ref, gpost_ref,
                y_ref, st_ref, carry, gbuf, vbuf, abuf, *, rows, subs, step, n_tiles, fold):
    shift = SUBLANES if fold else step
    halo = max(SUBLANES, (CONV_W - 1) * shift)
    taps = CONV_W - 1
    l = pl.program_id(1)

    @pl.when(l == 0)
    def _():
        carry[...] = jnp.zeros(carry.shape, _F32)
        if fold:
            for j in range(taps):
                carry[(j + 1) * SUBLANES - 1:(j + 1) * SUBLANES, :] = ctx_ref[0, j:j + 1, :]
        else:
            carry[halo - taps * step:halo, :] = ctx_ref[0]

    if fold:
        sub0 = lax.broadcasted_iota(jnp.int32, (SUBLANES, 1), 0) == 0
    for sb in range(subs):
        x = x_ref[0, sb * rows:(sb + 1) * rows, :]
        h = _rms(x, gpre_ref[...])
        if fold:
            h = _fold_rows(h)
        h = h.astype(_BF16)
        for c in range(D_FF // FF_CHUNK):
            conv = []
            for buf, off in ((gbuf, 0), (vbuf, D_FF)):
                cs = slice(off + c * FF_CHUNK, off + (c + 1) * FF_CHUNK)
                u = _dot(h, wup_ref[:, cs])
                if fold:
                    for j in range(taps):
                        grp = u[rows - (taps - j) * SUBLANES:rows - (taps - j - 1) * SUBLANES, :]
                        prev = carry[(j + 1) * SUBLANES - 1:(j + 1) * SUBLANES, cs]
                        buf[j * SUBLANES:(j + 1) * SUBLANES, :] = jnp.where(
                            sub0, prev, pltpu.roll(grp, 1, axis=0))
                    carry[:, cs] = u[rows - halo:rows, :]
                    buf[halo:halo + rows, :] = u
                else:
                    buf[0:halo, :] = carry[:, cs]
                    buf[halo:halo + rows, :] = u
                    carry[:, cs] = buf[rows:rows + halo, :]
                conv.append(cb_ref[:, cs]
                            + buf[halo - 2 * shift:halo - 2 * shift + rows, :] * cw_ref[0:1, cs]
                            + buf[halo - shift:halo - shift + rows, :] * cw_ref[1:2, cs]
                            + u * cw_ref[2:3, cs])
            a = jax.nn.gelu(conv[0], approximate=True) * conv[1]
            abuf[sb, :, c * FF_CHUNK:(c + 1) * FF_CHUNK] = a.astype(_BF16)
        m = _rms(_dot(abuf[sb], wdn_ref[...]), gpost_ref[...])
        if fold:
            m = _unfold_rows(m)
        y_ref[0, sb * rows:(sb + 1) * rows, :] = x + m

    @pl.when(l == n_tiles - 1)
    def _():
        if fold:
            for j in range(taps):
                st_ref[0, j:j + 1, :] = carry[(j + 1) * SUBLANES - 1:(j + 1) * SUBLANES, :]
        else:
            st_ref[0] = carry[halo - taps * step:halo, :]


def _ffn_layer(x, ctx, gpre, wup_bf16, cw, cb, wdn_bf16, gpost, *, rows, step, subs=1,
               fold=False):
    nb, n, _ = x.shape
    blk = rows * subs
    n_tiles = n // blk
    assert not fold or (step == 1 and rows % (SUBLANES * SUBLANES) == 0)
    halo = max(SUBLANES, (CONV_W - 1) * (SUBLANES if fold else step))
    ctx_rows = (CONV_W - 1) * step
    kern = functools.partial(_ffn_kernel, rows=rows, subs=subs, step=step, n_tiles=n_tiles,
                             fold=fold)
    return pl.pallas_call(
        kern,
        grid=(nb, n_tiles),
        in_specs=[
            pl.BlockSpec((1, blk, D_MODEL), lambda b, l: (b, l, 0)),
            pl.BlockSpec((1, ctx_rows, 2 * D_FF), lambda b, l: (b, 0, 0)),
            _const_spec((1, D_MODEL)),
            _const_spec((D_MODEL, 2 * D_FF)),
            _const_spec((CONV_W, 2 * D_FF)),
            _const_spec((1, 2 * D_FF)),
            _const_spec((D_FF, D_MODEL)),
            _const_spec((1, D_MODEL)),
        ],
        out_specs=[
            pl.BlockSpec((1, blk, D_MODEL), lambda b, l: (b, l, 0)),
            pl.BlockSpec((1, ctx_rows, 2 * D_FF), lambda b, l: (b, 0, 0)),
        ],
        out_shape=[
            jax.ShapeDtypeStruct(x.shape, _F32),
            jax.ShapeDtypeStruct((nb, ctx_rows, 2 * D_FF), _F32),
        ],
        scratch_shapes=[
            pltpu.VMEM((halo, 2 * D_FF), _F32),
            pltpu.VMEM((halo + rows, FF_CHUNK), _F32),
            pltpu.VMEM((halo + rows, FF_CHUNK), _F32),
            pltpu.VMEM((subs, rows, D_FF), _BF16),
        ],
        compiler_params=_params(),
        name="conv_ffn",
    )(x, ctx, gpre, wup_bf16, cw, cb, wdn_bf16, gpost)


def _silu(x):
    return x * (0.5 * jnp.tanh(0.5 * x) + 0.5)


def _hgrn_gates(h, win_ref, lbl_ref, q_ref, k_ref, lf_ref, v_ref, gs_ref):
    l0 = lbl_ref[0:1, :]
    l1 = lbl_ref[1:2, :]
    mx = jnp.maximum(l0, l1)
    e0 = jnp.exp(l0 - mx)
    e1 = jnp.exp(l1 - mx)
    lb_all = e1 / (e0 + e1)

    assert F_DIM == V_DIM
    for c in range(F_DIM // PROJ_CHUNK):
        cs = slice(c * PROJ_CHUNK, (c + 1) * PROJ_CHUNK)
        fr = _dot(h, win_ref[:, F_DIM + c * PROJ_CHUNK:F_DIM + (c + 1) * PROJ_CHUNK])
        off = 2 * F_DIM + c * PROJ_CHUNK
        v_ref[:, cs] = _dot(h, win_ref[:, off:off + PROJ_CHUNK]).astype(v_ref.dtype)
        qr = _dot(h, win_ref[:, cs])
        gr = _dot(h, win_ref[:, off + V_DIM:off + V_DIM + PROJ_CHUNK])
        e = jnp.exp(-jnp.abs(fr))
        big = 1.0 / (1.0 + e)
        small = e * big
        pos = fr >= 0.0
        lb = lb_all[:, cs]
        lf_ref[:, cs] = jnp.log(lb + (1.0 - lb) * jnp.where(pos, big, small))
        k_ref[:, cs] = (1.0 - lb) * jnp.where(pos, small, big)
        q_ref[:, cs] = _silu(qr) * (HEAD_K ** -0.5)
        gs_ref[:, cs] = _silu(gr)


def _hproj_kernel(x_ref, gpre_ref, win_ref, lbl_ref, q_ref, k_ref, lf_ref, v_ref, gs_ref):
    h = _rms(x_ref[...], gpre_ref[...]).astype(_BF16)
    _hgrn_gates(h, win_ref, lbl_ref, q_ref, k_ref, lf_ref, v_ref, gs_ref)


def _hgrn_proj(x2d, gpre, win_bf16, lb_logits, *, rows):
    n = x2d.shape[0]
    row_spec = pl.BlockSpec((rows, D_MODEL), lambda i: (i, 0))
    out = jax.ShapeDtypeStruct((n, D_MODEL), _F32)
    return pl.pallas_call(
        _hproj_kernel,
        grid=(n // rows,),
        in_specs=[row_spec, _const_spec((1, D_MODEL)), _const_spec(win_bf16.shape),
                  _const_spec(lb_logits.shape)],
        out_specs=[row_spec] * 5,
        out_shape=[out] * 5,
        compiler_params=pltpu.CompilerParams(
            dimension_semantics=("arbitrary",), vmem_limit_bytes=VMEM_LIMIT),
        name="hgrn_proj",
    )(x2d, gpre, win_bf16, lb_logits)


def _hout_kernel(x_ref, o_ref, gs_ref, gn_ref, wout_ref, gpost_ref, y_ref, zbuf):
    for hd in range(N_HEADS):
        hs = slice(hd * HEAD_V, (hd + 1) * HEAD_V)
        zbuf[:, hs] = (_rms(o_ref[:, hs], gn_ref[...]) * gs_ref[:, hs]).astype(_BF16)
    m = _dot(zbuf[...], wout_ref[...])
    y_ref[...] = x_ref[...] + _rms(m, gpost_ref[...])


def _hgrn_out(x2d, o2d, gs2d, gnorm, wout_bf16, gpost, *, rows):
    n = x2d.shape[0]
    row_spec = pl.BlockSpec((rows, D_MODEL), lambda i: (i, 0))
    return pl.pallas_call(
        _hout_kernel,
        grid=(n // rows,),
        in_specs=[row_spec, row_spec, row_spec, _const_spec((1, HEAD_V)),
                  _const_spec(wout_bf16.shape), _const_spec((1, D_MODEL))],
        out_specs=row_spec,
        out_shape=jax.ShapeDtypeStruct((n, D_MODEL), _F32),
        scratch_shapes=[pltpu.VMEM((rows, V_DIM), _BF16)],
        compiler_params=pltpu.CompilerParams(
            dimension_semantics=("arbitrary",), vmem_limit_bytes=VMEM_LIMIT),
        name="hgrn_out",
    )(x2d, o2d, gs2d, gnorm, wout_bf16, gpost)


def _gla_exponent_matrix():
    c = GLA_CHUNK
    t = np.arange(c)[:, None]
    w = np.arange(c)[None, :]
    blocks = [(w <= t)]
    for lev in range(1, GLA_FINE_LEVELS + 1):
        n = 1 << lev
        mid = (t // n) * n + n // 2 - 1
        later = (t > mid) & (w > mid) & (w <= t)
        earlier = (t <= mid) & (w > t) & (w <= mid)
        blocks.append(later | earlier)
    return np.concatenate(blocks, axis=0).astype(np.float32)


def _hgrn_prompt_kernel(x_ref, s0_ref, gpre_ref, win_ref, lbl_ref, mall_ref, gn_ref, wout_ref,
                        gpost_ref, y_ref, sn_ref, st, q_buf, k_buf, lf_buf, v_buf, gs_buf, z_buf,
                        *, rows, n_tiles):
    c = GLA_CHUNK
    l = pl.program_id(1)

    @pl.when(l == 0)
    def _():
        for hd in range(N_HEADS):
            st[hd] = s0_ref[0, hd].T

    x = x_ref[0]
    _hgrn_gates(_rms(x, gpre_ref[...]).astype(_BF16), win_ref, lbl_ref,
                q_buf, k_buf, lf_buf, v_buf, gs_buf)

    ti = lax.broadcasted_iota(jnp.int32, (c, c), 0)
    si = lax.broadcasted_iota(jnp.int32, (c, c), 1)
    ri = lax.broadcasted_iota(jnp.int32, (c, 1), 0)
    diag = ti == si
    lvl_mask, lvl_role = [], []
    for lev in range(1, GLA_LEVELS + 1):
        same = (ti >> lev) == (si >> lev)
        t_late = ((ti >> (lev - 1)) & 1) == 1
        s_early = ((si >> (lev - 1)) & 1) == 0
        lvl_mask.append(same & t_late & s_early)
        lvl_role.append(((ri >> (lev - 1)) & 1) == 1)

    def chunk(ci, carry):
        rs = pl.ds(pl.multiple_of(ci * c, c), c)
        lf = lf_buf[rs, :] * LOG2_E
        lf_hi = lf.astype(_BF16)
        r1 = lf - lf_hi.astype(_F32)
        lf_mid = r1.astype(_BF16)
        lf_lo = (r1 - lf_mid.astype(_F32)).astype(_BF16)
        cum = _dot(mall_ref[...], jnp.concatenate([lf_hi, lf_mid, lf_lo], axis=0))
        b = cum[0:c]
        eb = jnp.exp2(b)
        eb_rest = jnp.exp2(b[c - 1:c] - b)
        eb_last = eb[c - 1:c]
        q = q_buf[rs, :]
        k = k_buf[rs, :]
        vb = v_buf[rs, :]
        qd = (q * eb).astype(_BF16)
        kd = (k * eb_rest).astype(_BF16)
        lvl_x = []
        for lev in range(1, GLA_LEVELS + 1):
            n = 1 << lev
            if lev <= GLA_FINE_LEVELS:
                src = jnp.where(lvl_role[lev - 1], q, k)
                expo = cum[lev * c:(lev + 1) * c]
            else:
                src, expo = [], []
                for r0 in range(0, c, n):
                    mid = r0 + n // 2 - 1
                    src += [k[r0:mid + 1], q[mid + 1:r0 + n]]
                    expo += [b[mid:mid + 1] - b[r0:mid + 1], b[mid + 1:r0 + n] - b[mid:mid + 1]]
                src = jnp.concatenate(src, axis=0)
                expo = jnp.concatenate(expo, axis=0)
            lvl_x.append((src * jnp.exp2(expo)).astype(_BF16))
        qk = q * k
        heads = [slice(hd * HEAD_K, (hd + 1) * HEAD_K) for hd in range(N_HEADS)]
        scores = []
        for hs in heads:
            a = jnp.where(diag, jnp.sum(qk[:, hs], axis=-1, keepdims=True), 0.0)
            for lev in range(GLA_LEVELS):
                xl = lvl_x[lev][:, hs]
                a = jnp.where(lvl_mask[lev], _dot_nt(xl, xl), a)
            scores.append(a.astype(_BF16))
        s_old = [st[hd] for hd in range(N_HEADS)]
        for hd, hs in enumerate(heads):
            st[hd] = s_old[hd] * eb_last[:, hs] + _dot_tn(vb[:, hs], kd[:, hs])
        for hd, hs in enumerate(heads):
            o = _dot_nt(qd[:, hs], s_old[hd].astype(_BF16)) + _dot(scores[hd], vb[:, hs])
            z_buf[rs, hs] = (_rms(o, gn_ref[...]) * gs_buf[rs, hs]).astype(_BF16)
        return carry

    lax.fori_loop(0, rows // c, chunk, 0, unroll=GLA_UNROLL)

    m = _dot(z_buf[...], wout_ref[...])
    y_ref[0] = x + _rms(m, gpost_ref[...])

    @pl.when(l == n_tiles - 1)
    def _():
        for hd in range(N_HEADS):
            sn_ref[0, hd] = st[hd].T


def _hgrn_prompt(x, s0, gpre, win_bf16, lb_logits, gnorm, wout_bf16, gpost, *, rows):
    nb, n, _ = x.shape
    n_tiles = n // rows
    mall = jnp.asarray(np.tile(_gla_exponent_matrix(), (1, 3)), dtype=_BF16)
    row_spec = pl.BlockSpec((1, rows, D_MODEL), lambda b, l: (b, l, 0))
    st_spec = pl.BlockSpec((1, N_HEADS, HEAD_K, HEAD_V), lambda b, l: (b, 0, 0, 0))
    kern = functools.partial(_hgrn_prompt_kernel, rows=rows, n_tiles=n_tiles)
    return pl.pallas_call(
        kern,
        grid=(nb, n_tiles),
        in_specs=[row_spec, st_spec, _const_spec((1, D_MODEL)), _const_spec(win_bf16.shape),
                  _const_spec(lb_logits.shape), _const_spec(mall.shape), _const_spec((1, HEAD_V)),
                  _const_spec(wout_bf16.shape), _const_spec((1, D_MODEL))],
        out_specs=[row_spec, st_spec],
        out_shape=[jax.ShapeDtypeStruct(x.shape, _F32), jax.ShapeDtypeStruct(s0.shape, _F32)],
        scratch_shapes=[
            pltpu.VMEM((N_HEADS, HEAD_V, HEAD_K), _F32),
            pltpu.VMEM((rows, F_DIM), _F32),
            pltpu.VMEM((rows, F_DIM), _F32),
            pltpu.VMEM((rows, F_DIM), _F32),
            pltpu.VMEM((rows, V_DIM), _BF16),
            pltpu.VMEM((rows, V_DIM), _F32),
            pltpu.VMEM((rows, V_DIM), _BF16),
        ],
        compiler_params=_params(),
        name="hgrn_prompt",
    )(x, s0, gpre, win_bf16, lb_logits, mall, gnorm, wout_bf16, gpost)


def _gla_sample_kernel(q_ref, k_ref, lf_ref, v_ref, s0_ref, o_ref, sn_ref, *, steps):
    heads = [slice(hd * HEAD_K, (hd + 1) * HEAD_K) for hd in range(N_HEADS)]
    q = [q_ref[t] for t in range(steps)]
    k = [k_ref[t] for t in range(steps)]
    v = [v_ref[t] for t in range(steps)]
    b = [lf_ref[0]]
    for t in range(1, steps):
        b.append(b[-1] + lf_ref[t])
    b_last = b[-1]
    eb_last = jnp.exp(b_last)
    qd = [q[t] * jnp.exp(b[t]) for t in range(steps)]
    kd = [k[s] * jnp.exp(b_last - b[s]) for s in range(steps)]

    o_intra = []
    for t in range(steps):
        acc = None
        for s in range(t + 1):
            p = q[t] * k[s] if s == t else q[t] * k[s] * jnp.exp(b[t] - b[s])
            part = jnp.concatenate(
                [jnp.sum(p[:, hs], axis=-1, keepdims=True) * v[s][:, hs] for hs in heads], axis=-1)
            acc = part if acc is None else acc + part
        o_intra.append(acc)

    zero = jnp.zeros_like(b_last)
    rows_pad = [zero] * (SUBLANES - steps)
    e_hi = eb_last.astype(_BF16).astype(_F32)
    e_r = eb_last - e_hi
    e_mid = e_r.astype(_BF16).astype(_F32)
    e_lo = e_r - e_mid
    assert steps + 3 <= SUBLANES
    qd_b = jnp.stack(qd + rows_pad).swapaxes(0, 1).astype(_BF16)
    v_b = jnp.stack(v + rows_pad).swapaxes(0, 1)
    kd_b = jnp.stack(kd + [e_hi, e_mid, e_lo] + rows_pad[3:]).swapaxes(0, 1).astype(_BF16)
    ri = lax.broadcasted_iota(jnp.int32, (SUBLANES, HEAD_V), 0)
    ones_rows = ((ri >= steps) & (ri < steps + 3)).astype(_F32)

    o_inter = []
    for bi in range(SUBLANES):
        outs = []
        for hd, hs in enumerate(heads):
            s0 = s0_ref[bi, hd]
            outs.append(_dot(qd_b[bi][:, hs], s0.astype(_BF16)))
            rhs = jnp.concatenate([v_b[bi][:, hs], ones_rows], axis=-1).astype(_BF16)
            both = _dot_tn(kd_b[bi][:, hs], rhs)
            sn_ref[bi, hd] = both[:, HEAD_V:] * s0 + both[:, :HEAD_V]
        o_inter.append(jnp.concatenate(outs, axis=-1))
    o_inter = jnp.stack(o_inter).swapaxes(0, 1)
    for t in range(steps):
        o_ref[t] = o_intra[t] + o_inter[t]


def _gla_sample(q, k, lf, v, s0):
    steps, nb, _ = q.shape
    row_spec = pl.BlockSpec((steps, SUBLANES, D_MODEL), lambda i: (0, i, 0))
    st_spec = pl.BlockSpec((SUBLANES, N_HEADS, HEAD_K, HEAD_V), lambda i: (i, 0, 0, 0))
    kern = functools.partial(_gla_sample_kernel, steps=steps)
    return pl.pallas_call(
        kern,
        grid=(nb // SUBLANES,),
        in_specs=[row_spec, row_spec, row_spec, row_spec, st_spec],
        out_specs=[row_spec, st_spec],
        out_shape=[jax.ShapeDtypeStruct(q.shape, _F32), jax.ShapeDtypeStruct(s0.shape, _F32)],
        compiler_params=pltpu.CompilerParams(
            dimension_semantics=("arbitrary",), vmem_limit_bytes=VMEM_LIMIT),
        name="gla_sample",
    )(q, k, lf, v, s0)


def _row(v):
    return v.reshape(1, -1)


def _trunk_prompt(x, wts):
    nb, n, d = x.shape
    pool_ctx = jnp.zeros((nb, POOL_CTX, d), _F32)
    ffn_ctx = jnp.zeros((nb, CONV_W - 1, 2 * D_FF), _F32)
    s0 = jnp.zeros((nb, N_HEADS, HEAD_K, HEAD_V), _F32)

    x, pool_st = _pool_layer(x, pool_ctx, wts["mix_pre"][0], wts["pool_w"], wts["pool_scale"],
                             wts["mix_post"][0], rows=ROW_TILE, step=1, pos0=0)
    x, ffn_st0 = _ffn_layer(x, ffn_ctx, wts["ffn_pre"][0], wts["w_up"][0], wts["conv_w"][0],
                            wts["conv_b"][0], wts["w_down"][0], wts["ffn_post"][0],
                            rows=ROW_TILE, step=1, subs=FFN_SUBS, fold=True)
    x, s_new = _hgrn_prompt(x, s0, wts["mix_pre"][1], wts["w_in"], wts["lb_logits"], wts["gnorm"],
                            wts["w_out"], wts["mix_post"][1], rows=ROW_TILE)
    x, ffn_st1 = _ffn_layer(x, ffn_ctx, wts["ffn_pre"][1], wts["w_up"][1],
                            wts["conv_w"][1], wts["conv_b"][1], wts["w_down"][1],
                            wts["ffn_post"][1], rows=ROW_TILE, step=1, subs=FFN_SUBS, fold=True)
    return x, pool_st[None], s_new[None], jnp.stack([ffn_st0, ffn_st1])


def _trunk_sample(x, pos0, state_pool, state_hgrn, state_ffn, wts):
    nb, steps, d = x.shape
    n = nb * steps

    def to_time_major(a):
        return a.transpose(1, 0, 2).reshape(1, a.shape[1] * nb, a.shape[2])

    def to_batch_major(a, t):
        return a.reshape(t, nb, a.shape[-1]).transpose(1, 0, 2)

    xt = to_time_major(x)
    xt, pool_st = _pool_layer(xt, to_time_major(state_pool[0]), wts["mix_pre"][0], wts["pool_w"],
                              wts["pool_scale"], wts["mix_post"][0], rows=n, step=nb, pos0=pos0)
    xt, ffn_st0 = _ffn_layer(xt, to_time_major(state_ffn[0]), wts["ffn_pre"][0], wts["w_up"][0],
                             wts["conv_w"][0], wts["conv_b"][0], wts["w_down"][0],
                             wts["ffn_post"][0], rows=n, step=nb)
    x2d = xt.reshape(n, d)
    proj = _hgrn_proj(x2d, wts["mix_pre"][1], wts["w_in"], wts["lb_logits"], rows=n)

    q, k, lf, v = (a.reshape(steps, nb, d) for a in proj[:4])
    o, s_new = _gla_sample(q, k, lf, v, state_hgrn[0])
    x2d = _hgrn_out(x2d, o.reshape(n, d), proj[4], wts["gnorm"], wts["w_out"],
                    wts["mix_post"][1], rows=n)
    xt, ffn_st1 = _ffn_layer(x2d.reshape(1, n, d), to_time_major(state_ffn[1]), wts["ffn_pre"][1],
                             wts["w_up"][1], wts["conv_w"][1], wts["conv_b"][1],
                             wts["w_down"][1], wts["ffn_post"][1], rows=n, step=nb)
    y = to_batch_major(xt, steps)
    new_pool = to_batch_major(pool_st, POOL_CTX)[None]
    new_ffn = jnp.stack([to_batch_major(ffn_st0, CONV_W - 1), to_batch_major(ffn_st1, CONV_W - 1)])
    return y, new_pool, s_new[None], new_ffn


def kernel(x_prompt, x_sample, state_pool, state_hgrn, state_ffn_conv, norm_mix_pre, norm_mix_post,
           norm_ffn_pre, norm_ffn_post, pool_w, pool_scale, hgrn_w_in, hgrn_lb_logits, hgrn_gnorm,
           hgrn_w_out, ffn_w_up, ffn_conv_w, ffn_conv_b, ffn_w_down):
    depth = ffn_w_up.shape[0]
    assert depth == 2 and pool_w.shape[0] == 1 and hgrn_w_in.shape[0] == 1
    past_len = 16384
    wts = {
        "mix_pre": [_row(norm_mix_pre[i]) for i in range(depth)],
        "mix_post": [_row(norm_mix_post[i]) for i in range(depth)],
        "ffn_pre": [_row(norm_ffn_pre[i]) for i in range(depth)],
        "ffn_post": [_row(norm_ffn_post[i]) for i in range(depth)],
        "pool_w": pool_w[0].astype(_BF16),
        "pool_scale": _row(pool_scale[0]),
        "w_in": hgrn_w_in[0].astype(_BF16),
        "lb_logits": hgrn_lb_logits,
        "gnorm": _row(hgrn_gnorm[0]),
        "w_out": hgrn_w_out[0].astype(_BF16),
        "w_up": [ffn_w_up[i].astype(_BF16) for i in range(depth)],
        "conv_w": [ffn_conv_w[i] for i in range(depth)],
        "conv_b": [_row(ffn_conv_b[i]) for i in range(depth)],
        "w_down": [ffn_w_down[i].astype(_BF16) for i in range(depth)],
    }
    y_p, pool_p, hgrn_p, ffn_p = _trunk_prompt(x_prompt, wts)
    y_s, pool_s, hgrn_s, ffn_s = _trunk_sample(x_sample, past_len, state_pool, state_hgrn,
                                               state_ffn_conv, wts)
    return (y_p, y_s, pool_p, pool_s, hgrn_p, hgrn_s, ffn_p, ffn_s)
```

```python
import functools

import jax
import jax.numpy as jnp
import numpy as np
from jax import lax
from jax.experimental import pallas as pl
from jax.experimental.pallas import tpu as pltpu

D_MODEL = 1024
POOL_WINDOWS = (2, 4, 8, 16)
POOL_GROUP_DIM = D_MODEL // len(POOL_WINDOWS)
POOL_CTX = max(POOL_WINDOWS) - 1
N_HEADS = 8
HEAD_K = 128
HEAD_V = D_MODEL // N_HEADS
F_DIM = N_HEADS * HEAD_K
V_DIM = N_HEADS * HEAD_V
D_FF = 2816
CONV_W = 3
EPS = 1e-6
LOG2_E = 1.4426950408889634

SUBLANES = 8
ROW_TILE = 512
FF_CHUNK = 256
PROJ_CHUNK = 256
GLA_CHUNK = 64
GLA_LEVELS = 6
GLA_FINE_LEVELS = 3
GLA_UNROLL = 2
VMEM_LIMIT = 56 * 1024 * 1024

_F32 = jnp.float32
_BF16 = jnp.bfloat16


def _rms(x, g):
    ms = jnp.mean(x * x, axis=-1, keepdims=True)
    return x * lax.rsqrt(ms + EPS) * g


def _dot(a, b):
    return jnp.dot(a, b, preferred_element_type=_F32)


def _dot_nt(a, b):
    return lax.dot_general(a, b, (((1,), (1,)), ((), ())), preferred_element_type=_F32)


def _dot_tn(a, b):
    return lax.dot_general(a, b, (((0,), (0,)), ((), ())), preferred_element_type=_F32)


def _const_spec(shape):
    nd = len(shape)
    return pl.BlockSpec(shape, lambda *_: (0,) * nd, pipeline_mode=pl.Buffered(1))


def _params():
    return pltpu.CompilerParams(
        dimension_semantics=("arbitrary", "arbitrary"), vmem_limit_bytes=VMEM_LIMIT)


def _fold_rows(a):
    rows, c = a.shape
    return a.reshape(SUBLANES, rows // SUBLANES, c).swapaxes(0, 1).reshape(rows, c)


def _unfold_rows(a):
    rows, c = a.shape
    return a.reshape(rows // SUBLANES, SUBLANES, c).swapaxes(0, 1).reshape(rows, c)


def _pool_kernel(x_ref, ctx_ref, gpre_ref, w_ref, scale_ref, gpost_ref, y_ref, st_ref, hbuf,
                 *, rows, step, pos0, n_tiles):
    halo = (POOL_CTX + 1) * step
    l = pl.program_id(1)

    @pl.when(l == 0)
    def _():
        hbuf[0:step, :] = jnp.zeros((step, D_MODEL), _F32)
        hbuf[step:halo, :] = ctx_ref[0]

    x = x_ref[0]
    h = _rms(x, gpre_ref[...])
    hbuf[halo:halo + rows, :] = h

    if pos0 < POOL_CTX:
        assert step == 1
        pos = pos0 + l * rows + lax.broadcasted_iota(jnp.int32, (rows, 1), 0)
    outs = []
    for g, w in enumerate(POOL_WINDOWS):
        cs = slice(g * POOL_GROUP_DIM, (g + 1) * POOL_GROUP_DIM)
        s = hbuf[:, cs]
        span = 1
        while span < w:
            s = s + pltpu.roll(s, span * step, axis=0)
            span *= 2
        s = s[halo:halo + rows]
        if pos0 < POOL_CTX:
            cnt = jnp.minimum(pos + 1, w).astype(_F32)
        else:
            cnt = float(w)
        p = s / cnt - h[:, cs]
        outs.append(_dot(p.astype(_BF16), w_ref[g]))
    m = jnp.concatenate(outs, axis=-1) * scale_ref[...]
    y_ref[0] = x + _rms(m, gpost_ref[...])

    @pl.when(l == n_tiles - 1)
    def _():
        st_ref[0] = hbuf[rows + step:rows + halo, :]

    if n_tiles > 1:
        hbuf[0:halo, :] = hbuf[rows:rows + halo, :]


def _pool_layer(x, ctx, gpre, w_bf16, scale, gpost, *, rows, step, pos0):
    nb, n, _ = x.shape
    n_tiles = n // rows
    halo = (POOL_CTX + 1) * step
    ctx_rows = POOL_CTX * step
    kern = functools.partial(_pool_kernel, rows=rows, step=step, pos0=pos0, n_tiles=n_tiles)
    return pl.pallas_call(
        kern,
        grid=(nb, n_tiles),
        in_specs=[
            pl.BlockSpec((1, rows, D_MODEL), lambda b, l: (b, l, 0)),
            pl.BlockSpec((1, ctx_rows, D_MODEL), lambda b, l: (b, 0, 0)),
            _const_spec((1, D_MODEL)),
            _const_spec(w_bf16.shape),
            _const_spec((1, D_MODEL)),
            _const_spec((1, D_MODEL)),
        ],
        out_specs=[
            pl.BlockSpec((1, rows, D_MODEL), lambda b, l: (b, l, 0)),
            pl.BlockSpec((1, ctx_rows, D_MODEL), lambda b, l: (b, 0, 0)),
        ],
        out_shape=[
            jax.ShapeDtypeStruct(x.shape, _F32),
            jax.ShapeDtypeStruct((nb, ctx_rows, D_MODEL), _F32),
        ],
        scratch_shapes=[pltpu.VMEM((halo + rows, D_MODEL), _F32)],
        compiler_params=_params(),
        name="pool_mixer",
    )(x, ctx, gpre, w_bf16, scale, gpost)


def _ffn_kernel(x_ref, ctx_ref, gpre_ref, wup_ref, cw_ref, cb_ref, wdn_ref, gpost_ref,
                y_ref, st_ref, carry, gbuf, vbuf, abuf, *, rows, step, n_tiles, fold):
    shift = SUBLANES if fold else step
    halo = max(SUBLANES, (CONV_W - 1) * shift)
    taps = CONV_W - 1
    l = pl.program_id(1)

    @pl.when(l == 0)
    def _():
        carry[...] = jnp.zeros(carry.shape, _F32)
        if fold:
            for j in range(taps):
                carry[(j + 1) * SUBLANES - 1:(j + 1) * SUBLANES, :] = ctx_ref[0, j:j + 1, :]
        else:
            carry[halo - taps * step:halo, :] = ctx_ref[0]

    x = x_ref[0]
    h = _rms(x, gpre_ref[...])
    if fold:
        h = _fold_rows(h)
        sub0 = lax.broadcasted_iota(jnp.int32, (SUBLANES, 1), 0) == 0
    h = h.astype(_BF16)
    for c in range(D_FF // FF_CHUNK):
        conv = []
        for buf, off in ((gbuf, 0), (vbuf, D_FF)):
            cs = slice(off + c * FF_CHUNK, off + (c + 1) * FF_CHUNK)
            u = _dot(h, wup_ref[:, cs])
            if fold:
                for j in range(taps):
                    grp = u[rows - (taps - j) * SUBLANES:rows - (taps - j - 1) * SUBLANES, :]
                    prev = carry[(j + 1) * SUBLANES - 1:(j + 1) * SUBLANES, cs]
                    buf[j * SUBLANES:(j + 1) * SUBLANES, :] = jnp.where(
                        sub0, prev, pltpu.roll(grp, 1, axis=0))
                carry[:, cs] = u[rows - halo:rows, :]
                buf[halo:halo + rows, :] = u
            else:
                buf[0:halo, :] = carry[:, cs]
                buf[halo:halo + rows, :] = u
                carry[:, cs] = buf[rows:rows + halo, :]
            conv.append(cb_ref[:, cs]
                        + buf[halo - 2 * shift:halo - 2 * shift + rows, :] * cw_ref[0:1, cs]
                        + buf[halo - shift:halo - shift + rows, :] * cw_ref[1:2, cs]
                        + u * cw_ref[2:3, cs])
        a = jax.nn.gelu(conv[0], approximate=True) * conv[1]
        abuf[:, c * FF_CHUNK:(c + 1) * FF_CHUNK] = a.astype(_BF16)
    m = _rms(_dot(abuf[...], wdn_ref[...]), gpost_ref[...])
    if fold:
        m = _unfold_rows(m)
    y_ref[0] = x + m

    @pl.when(l == n_tiles - 1)
    def _():
        if fold:
            for j in range(taps):
                st_ref[0, j:j + 1, :] = carry[(j + 1) * SUBLANES - 1:(j + 1) * SUBLANES, :]
        else:
            st_ref[0] = carry[halo - taps * step:halo, :]


def _ffn_layer(x, ctx, gpre, wup_bf16, cw, cb, wdn_bf16, gpost, *, rows, step, fold=False):
    nb, n, _ = x.shape
    n_tiles = n // rows
    assert not fold or (step == 1 and rows % (SUBLANES * SUBLANES) == 0)
    halo = max(SUBLANES, (CONV_W - 1) * (SUBLANES if fold else step))
    ctx_rows = (CONV_W - 1) * step
    kern = functools.partial(_ffn_kernel, rows=rows, step=step, n_tiles=n_tiles, fold=fold)
    return pl.pallas_call(
        kern,
        grid=(nb, n_tiles),
        in_specs=[
            pl.BlockSpec((1, rows, D_MODEL), lambda b, l: (b, l, 0)),
            pl.BlockSpec((1, ctx_rows, 2 * D_FF), lambda b, l: (b, 0, 0)),
            _const_spec((1, D_MODEL)),
            _const_spec((D_MODEL, 2 * D_FF)),
            _const_spec((CONV_W, 2 * D_FF)),
            _const_spec((1, 2 * D_FF)),
            _const_spec((D_FF, D_MODEL)),
            _const_spec((1, D_MODEL)),
        ],
        out_specs=[
            pl.BlockSpec((1, rows, D_MODEL), lambda b, l: (b, l, 0)),
            pl.BlockSpec((1, ctx_rows, 2 * D_FF), lambda b, l: (b, 0, 0)),
        ],
        out_shape=[
            jax.ShapeDtypeStruct(x.shape, _F32),
            jax.ShapeDtypeStruct((nb, ctx_rows, 2 * D_FF), _F32),
        ],
        scratch_shapes=[
            pltpu.VMEM((halo, 2 * D_FF), _F32),
            pltpu.VMEM((halo + rows, FF_CHUNK), _F32),
            pltpu.VMEM((halo + rows, FF_CHUNK), _F32),
            pltpu.VMEM((rows, D_FF), _BF16),
        ],
        compiler_params=_params(),
        name="conv_ffn",
    )(x, ctx, gpre, wup_bf16, cw, cb, wdn_bf16, gpost)


def _silu(x):
    return x * (0.5 * jnp.tanh(0.5 * x) + 0.5)


def _hgrn_gates(h, win_ref, lbl_ref, q_ref, k_ref, lf_ref, v_ref, gs_ref):
    l0 = lbl_ref[0:1, :]
    l1 = lbl_ref[1:2, :]
    mx = jnp.maximum(l0, l1)
    e0 = jnp.exp(l0 - mx)
    e1 = jnp.exp(l1 - mx)
    lb_all = e1 / (e0 + e1)

    assert F_DIM == V_DIM
    for c in range(F_DIM // PROJ_CHUNK):
        cs = slice(c * PROJ_CHUNK, (c + 1) * PROJ_CHUNK)
        fr = _dot(h, win_ref[:, F_DIM + c * PROJ_CHUNK:F_DIM + (c + 1) * PROJ_CHUNK])
        off = 2 * F_DIM + c * PROJ_CHUNK
        v_ref[:, cs] = _dot(h, win_ref[:, off:off + PROJ_CHUNK]).astype(v_ref.dtype)
        qr = _dot(h, win_ref[:, cs])
        gr = _dot(h, win_ref[:, off + V_DIM:off + V_DIM + PROJ_CHUNK])
        e = jnp.exp(-jnp.abs(fr))
        big = 1.0 / (1.0 + e)
        small = e * big
        pos = fr >= 0.0
        lb = lb_all[:, cs]
        lf_ref[:, cs] = jnp.log(lb + (1.0 - lb) * jnp.where(pos, big, small))
        k_ref[:, cs] = (1.0 - lb) * jnp.where(pos, small, big)
        q_ref[:, cs] = _silu(qr) * (HEAD_K ** -0.5)
        gs_ref[:, cs] = _silu(gr)


def _hproj_kernel(x_ref, gpre_ref, win_ref, lbl_ref, q_ref, k_ref, lf_ref, v_ref, gs_ref):
    h = _rms(x_ref[...], gpre_ref[...]).astype(_BF16)
    _hgrn_gates(h, win_ref, lbl_ref, q_ref, k_ref, lf_ref, v_ref, gs_ref)


def _hgrn_proj(x2d, gpre, win_bf16, lb_logits, *, rows):
    n = x2d.shape[0]
    row_spec = pl.BlockSpec((rows, D_MODEL), lambda i: (i, 0))
    out = jax.ShapeDtypeStruct((n, D_MODEL), _F32)
    return pl.pallas_call(
        _hproj_kernel,
        grid=(n // rows,),
        in_specs=[row_spec, _const_spec((1, D_MODEL)), _const_spec(win_bf16.shape),
                  _const_spec(lb_logits.shape)],
        out_specs=[row_spec] * 5,
        out_shape=[out] * 5,
        compiler_params=pltpu.CompilerParams(
            dimension_semantics=("arbitrary",), vmem_limit_bytes=VMEM_LIMIT),
        name="hgrn_proj",
    )(x2d, gpre, win_bf16, lb_logits)


def _hout_kernel(x_ref, o_ref, gs_ref, gn_ref, wout_ref, gpost_ref, y_ref, zbuf):
    for hd in range(N_HEADS):
        hs = slice(hd * HEAD_V, (hd + 1) * HEAD_V)
        zbuf[:, hs] = (_rms(o_ref[:, hs], gn_ref[...]) * gs_ref[:, hs]).astype(_BF16)
    m = _dot(zbuf[...], wout_ref[...])
    y_ref[...] = x_ref[...] + _rms(m, gpost_ref[...])


def _hgrn_out(x2d, o2d, gs2d, gnorm, wout_bf16, gpost, *, rows):
    n = x2d.shape[0]
    row_spec = pl.BlockSpec((rows, D_MODEL), lambda i: (i, 0))
    return pl.pallas_call(
        _hout_kernel,
        grid=(n // rows,),
        in_specs=[row_spec, row_spec, row_spec, _const_spec((1, HEAD_V)),
                  _const_spec(wout_bf16.shape), _const_spec((1, D_MODEL))],
        out_specs=row_spec,
        out_shape=jax.ShapeDtypeStruct((n, D_MODEL), _F32),
        scratch_shapes=[pltpu.VMEM((rows, V_DIM), _BF16)],
        compiler_params=pltpu.CompilerParams(
            dimension_semantics=("arbitrary",), vmem_limit_bytes=VMEM_LIMIT),
        name="hgrn_out",
    )(x2d, o2d, gs2d, gnorm, wout_bf16, gpost)


def _gla_exponent_matrix():
    c = GLA_CHUNK
    t = np.arange(c)[:, None]
    w = np.arange(c)[None, :]
    blocks = [(w <= t)]
    for lev in range(1, GLA_FINE_LEVELS + 1):
        n = 1 << lev
        mid = (t // n) * n + n // 2 - 1
        later = (t > mid) & (w > mid) & (w <= t)
        earlier = (t <= mid) & (w > t) & (w <= mid)
        blocks.append(later | earlier)
    return np.concatenate(blocks, axis=0).astype(np.float32)


def _hgrn_prompt_kernel(x_ref, s0_ref, gpre_ref, win_ref, lbl_ref, mall_ref, gn_ref, wout_ref,
                        gpost_ref, y_ref, sn_ref, st, q_buf, k_buf, lf_buf, v_buf, gs_buf, z_buf,
                        *, rows, n_tiles):
    c = GLA_CHUNK
    l = pl.program_id(1)

    @pl.when(l == 0)
    def _():
        for hd in range(N_HEADS):
            st[hd] = s0_ref[0, hd].T

    x = x_ref[0]
    _hgrn_gates(_rms(x, gpre_ref[...]).astype(_BF16), win_ref, lbl_ref,
                q_buf, k_buf, lf_buf, v_buf, gs_buf)

    ti = lax.broadcasted_iota(jnp.int32, (c, c), 0)
    si = lax.broadcasted_iota(jnp.int32, (c, c), 1)
    ri = lax.broadcasted_iota(jnp.int32, (c, 1), 0)
    diag = ti == si
    lvl_mask, lvl_role = [], []
    for lev in range(1, GLA_LEVELS + 1):
        same = (ti >> lev) == (si >> lev)
        t_late = ((ti >> (lev - 1)) & 1) == 1
        s_early = ((si >> (lev - 1)) & 1) == 0
        lvl_mask.append(same & t_late & s_early)
        lvl_role.append(((ri >> (lev - 1)) & 1) == 1)

    def chunk(ci, carry):
        rs = pl.ds(pl.multiple_of(ci * c, c), c)
        lf = lf_buf[rs, :] * LOG2_E
        lf_hi = lf.astype(_BF16)
        r1 = lf - lf_hi.astype(_F32)
        lf_mid = r1.astype(_BF16)
        lf_lo = (r1 - lf_mid.astype(_F32)).astype(_BF16)
        cum = _dot(mall_ref[...], jnp.concatenate([lf_hi, lf_mid, lf_lo], axis=0))
        b = cum[0:c]
        eb = jnp.exp2(b)
        eb_rest = jnp.exp2(b[c - 1:c] - b)
        eb_last = eb[c - 1:c]
        q = q_buf[rs, :]
        k = k_buf[rs, :]
        vb = v_buf[rs, :]
        qd = (q * eb).astype(_BF16)
        kd = (k * eb_rest).astype(_BF16)
        lvl_x = []
        for lev in range(1, GLA_LEVELS + 1):
            n = 1 << lev
            if lev <= GLA_FINE_LEVELS:
                src = jnp.where(lvl_role[lev - 1], q, k)
                expo = cum[lev * c:(lev + 1) * c]
            else:
                src, expo = [], []
                for r0 in range(0, c, n):
                    mid = r0 + n // 2 - 1
                    src += [k[r0:mid + 1], q[mid + 1:r0 + n]]
                    expo += [b[mid:mid + 1] - b[r0:mid + 1], b[mid + 1:r0 + n] - b[mid:mid + 1]]
                src = jnp.concatenate(src, axis=0)
                expo = jnp.concatenate(expo, axis=0)
            lvl_x.append((src * jnp.exp2(expo)).astype(_BF16))
        qk = q * k
        heads = [slice(hd * HEAD_K, (hd + 1) * HEAD_K) for hd in range(N_HEADS)]
        scores = []
        for hs in heads:
            a = jnp.where(diag, jnp.sum(qk[:, hs], axis=-1, keepdims=True), 0.0)
            for lev in range(GLA_LEVELS):
                xl = lvl_x[lev][:, hs]
                a = jnp.where(lvl_mask[lev], _dot_nt(xl, xl), a)
            scores.append(a.astype(_BF16))
        s_old = [st[hd] for hd in range(N_HEADS)]
        for hd, hs in enumerate(heads):
            st[hd] = s_old[hd] * eb_last[:, hs] + _dot_tn(vb[:, hs], kd[:, hs])
        for hd, hs in enumerate(heads):
            o = _dot_nt(qd[:, hs], s_old[hd].astype(_BF16)) + _dot(scores[hd], vb[:, hs])
            z_buf[rs, hs] = (_rms(o, gn_ref[...]) * gs_buf[rs, hs]).astype(_BF16)
        return carry

    lax.fori_loop(0, rows // c, chunk, 0, unroll=GLA_UNROLL)

    m = _dot(z_buf[...], wout_ref[...])
    y_ref[0] = x + _rms(m, gpost_ref[...])

    @pl.when(l == n_tiles - 1)
    def _():
        for hd in range(N_HEADS):
            sn_ref[0, hd] = st[hd].T


def _hgrn_prompt(x, s0, gpre, win_bf16, lb_logits, gnorm, wout_bf16, gpost, *, rows):
    nb, n, _ = x.shape
    n_tiles = n // rows
    mall = jnp.asarray(np.tile(_gla_exponent_matrix(), (1, 3)), dtype=_BF16)
    row_spec = pl.BlockSpec((1, rows, D_MODEL), lambda b, l: (b, l, 0))
    st_spec = pl.BlockSpec((1, N_HEADS, HEAD_K, HEAD_V), lambda b, l: (b, 0, 0, 0))
    kern = functools.partial(_hgrn_prompt_kernel, rows=rows, n_tiles=n_tiles)
    return pl.pallas_call(
        kern,
        grid=(nb, n_tiles),
        in_specs=[row_spec, st_spec, _const_spec((1, D_MODEL)), _const_spec(win_bf16.shape),
                  _const_spec(lb_logits.shape), _const_spec(mall.shape), _const_spec((1, HEAD_V)),
                  _const_spec(wout_bf16.shape), _const_spec((1, D_MODEL))],
        out_specs=[row_spec, st_spec],
        out_shape=[jax.ShapeDtypeStruct(x.shape, _F32), jax.ShapeDtypeStruct(s0.shape, _F32)],
        scratch_shapes=[
            pltpu.VMEM((N_HEADS, HEAD_V, HEAD_K), _F32),
            pltpu.VMEM((rows, F_DIM), _F32),
            pltpu.VMEM((rows, F_DIM), _F32),
            pltpu.VMEM((rows, F_DIM), _F32),
            pltpu.VMEM((rows, V_DIM), _BF16),
            pltpu.VMEM((rows, V_DIM), _F32),
            pltpu.VMEM((rows, V_DIM), _BF16),
        ],
        compiler_params=_params(),
        name="hgrn_prompt",
    )(x, s0, gpre, win_bf16, lb_logits, mall, gnorm, wout_bf16, gpost)


def _gla_sample_kernel(q_ref, k_ref, lf_ref, v_ref, s0_ref, o_ref, sn_ref, *, steps):
    heads = [slice(hd * HEAD_K, (hd + 1) * HEAD_K) for hd in range(N_HEADS)]
    q = [q_ref[t] for t in range(steps)]
    k = [k_ref[t] for t in range(steps)]
    v = [v_ref[t] for t in range(steps)]
    b = [lf_ref[0]]
    for t in range(1, steps):
        b.append(b[-1] + lf_ref[t])
    b_last = b[-1]
    eb_last = jnp.exp(b_last)
    qd = [q[t] * jnp.exp(b[t]) for t in range(steps)]
    kd = [k[s] * jnp.exp(b_last - b[s]) for s in range(steps)]

    o_intra = []
    for t in range(steps):
        acc = None
        for s in range(t + 1):
            p = q[t] * k[s] if s == t else q[t] * k[s] * jnp.exp(b[t] - b[s])
            part = jnp.concatenate(
                [jnp.sum(p[:, hs], axis=-1, keepdims=True) * v[s][:, hs] for hs in heads], axis=-1)
            acc = part if acc is None else acc + part
        o_intra.append(acc)

    zero = jnp.zeros_like(b_last)
    rows_pad = [zero] * (SUBLANES - steps)
    e_hi = eb_last.astype(_BF16).astype(_F32)
    e_r = eb_last - e_hi
    e_mid = e_r.astype(_BF16).astype(_F32)
    e_lo = e_r - e_mid
    assert steps + 3 <= SUBLANES
    qd_b = jnp.stack(qd + rows_pad).swapaxes(0, 1).astype(_BF16)
    v_b = jnp.stack(v + rows_pad).swapaxes(0, 1)
    kd_b = jnp.stack(kd + [e_hi, e_mid, e_lo] + rows_pad[3:]).swapaxes(0, 1).astype(_BF16)
    ri = lax.broadcasted_iota(jnp.int32, (SUBLANES, HEAD_V), 0)
    ones_rows = ((ri >= steps) & (ri < steps + 3)).astype(_F32)

    o_inter = []
    for bi in range(SUBLANES):
        outs = []
        for hd, hs in enumerate(heads):
            s0 = s0_ref[bi, hd]
            outs.append(_dot(qd_b[bi][:, hs], s0.astype(_BF16)))
            rhs = jnp.concatenate([v_b[bi][:, hs], ones_rows], axis=-1).astype(_BF16)
            both = _dot_tn(kd_b[bi][:, hs], rhs)
            sn_ref[bi, hd] = both[:, HEAD_V:] * s0 + both[:, :HEAD_V]
        o_inter.append(jnp.concatenate(outs, axis=-1))
    o_inter = jnp.stack(o_inter).swapaxes(0, 1)
    for t in range(steps):
        o_ref[t] = o_intra[t] + o_inter[t]


def _gla_sample(q, k, lf, v, s0):
    steps, nb, _ = q.shape
    row_spec = pl.BlockSpec((steps, SUBLANES, D_MODEL), lambda i: (0, i, 0))
    st_spec = pl.BlockSpec((SUBLANES, N_HEADS, HEAD_K, HEAD_V), lambda i: (i, 0, 0, 0))
    kern = functools.partial(_gla_sample_kernel, steps=steps)
    return pl.pallas_call(
        kern,
        grid=(nb // SUBLANES,),
        in_specs=[row_spec, row_spec, row_spec, row_spec, st_spec],
        out_specs=[row_spec, st_spec],
        out_shape=[jax.ShapeDtypeStruct(q.shape, _F32), jax.ShapeDtypeStruct(s0.shape, _F32)],
        compiler_params=pltpu.CompilerParams(
            dimension_semantics=("arbitrary",), vmem_limit_bytes=VMEM_LIMIT),
        name="gla_sample",
    )(q, k, lf, v, s0)


def _row(v):
    return v.reshape(1, -1)


def _trunk_prompt(x, wts):
    nb, n, d = x.shape
    pool_ctx = jnp.zeros((nb, POOL_CTX, d), _F32)
    ffn_ctx = jnp.zeros((nb, CONV_W - 1, 2 * D_FF), _F32)
    s0 = jnp.zeros((nb, N_HEADS, HEAD_K, HEAD_V), _F32)

    x, pool_st = _pool_layer(x, pool_ctx, wts["mix_pre"][0], wts["pool_w"], wts["pool_scale"],
                             wts["mix_post"][0], rows=ROW_TILE, step=1, pos0=0)
    x, ffn_st0 = _ffn_layer(x, ffn_ctx, wts["ffn_pre"][0], wts["w_up"][0], wts["conv_w"][0],
                            wts["conv_b"][0], wts["w_down"][0], wts["ffn_post"][0],
                            rows=ROW_TILE, step=1, fold=True)
    x, s_new = _hgrn_prompt(x, s0, wts["mix_pre"][1], wts["w_in"], wts["lb_logits"], wts["gnorm"],
                            wts["w_out"], wts["mix_post"][1], rows=ROW_TILE)
    x, ffn_st1 = _ffn_layer(x, ffn_ctx, wts["ffn_pre"][1], wts["w_up"][1],
                            wts["conv_w"][1], wts["conv_b"][1], wts["w_down"][1],
                            wts["ffn_post"][1], rows=ROW_TILE, step=1, fold=True)
    return x, pool_st[None], s_new[None], jnp.stack([ffn_st0, ffn_st1])


def _trunk_sample(x, pos0, state_pool, state_hgrn, state_ffn, wts):
    nb, steps, d = x.shape
    n = nb * steps

    def to_time_major(a):
        return a.transpose(1, 0, 2).reshape(1, a.shape[1] * nb, a.shape[2])

    def to_batch_major(a, t):
        return a.reshape(t, nb, a.shape[-1]).transpose(1, 0, 2)

    xt = to_time_major(x)
    xt, pool_st = _pool_layer(xt, to_time_major(state_pool[0]), wts["mix_pre"][0], wts["pool_w"],
                              wts["pool_scale"], wts["mix_post"][0], rows=n, step=nb, pos0=pos0)
    xt, ffn_st0 = _ffn_layer(xt, to_time_major(state_ffn[0]), wts["ffn_pre"][0], wts["w_up"][0],
                             wts["conv_w"][0], wts["conv_b"][0], wts["w_down"][0],
                             wts["ffn_post"][0], rows=n, step=nb)
    x2d = xt.reshape(n, d)
    proj = _hgrn_proj(x2d, wts["mix_pre"][1], wts["w_in"], wts["lb_logits"], rows=n)

    q, k, lf, v = (a.reshape(steps, nb, d) for a in proj[:4])
    o, s_new = _gla_sample(q, k, lf, v, state_hgrn[0])
    x2d = _hgrn_out(x2d, o.reshape(n, d), proj[4], wts["gnorm"], wts["w_out"],
                    wts["mix_post"][1], rows=n)
    xt, ffn_st1 = _ffn_layer(x2d.reshape(1, n, d), to_time_major(state_ffn[1]), wts["ffn_pre"][1],
                             wts["w_up"][1], wts["conv_w"][1], wts["conv_b"][1],
                             wts["w_down"][1], wts["ffn_post"][1], rows=n, step=nb)
    y = to_batch_major(xt, steps)
    new_pool = to_batch_major(pool_st, POOL_CTX)[None]
    new_ffn = jnp.stack([to_batch_major(ffn_st0, CONV_W - 1), to_batch_major(ffn_st1, CONV_W - 1)])
    return y, new_pool, s_new[None], new_ffn


def kernel(x_prompt, x_sample, state_pool, state_hgrn, state_ffn_conv, norm_mix_pre, norm_mix_post,
           norm_ffn_pre, norm_ffn_post, pool_w, pool_scale, hgrn_w_in, hgrn_lb_logits, hgrn_gnorm,
           hgrn_w_out, ffn_w_up, ffn_conv_w, ffn_conv_b, ffn_w_down):
    depth = ffn_w_up.shape[0]
    assert depth == 2 and pool_w.shape[0] == 1 and hgrn_w_in.shape[0] == 1
    past_len = 16384
    wts = {
        "mix_pre": [_row(norm_mix_pre[i]) for i in range(depth)],
        "mix_post": [_row(norm_mix_post[i]) for i in range(depth)],
        "ffn_pre": [_row(norm_ffn_pre[i]) for i in range(depth)],
        "ffn_post": [_row(norm_ffn_post[i]) for i in range(depth)],
        "pool_w": pool_w[0].astype(_BF16),
        "pool_scale": _row(pool_scale[0]),
        "w_in": hgrn_w_in[0].astype(_BF16),
        "lb_logits": hgrn_lb_logits,
        "gnorm": _row(hgrn_gnorm[0]),
        "w_out": hgrn_w_out[0].astype(_BF16),
        "w_up": [ffn_w_up[i].astype(_BF16) for i in range(depth)],
        "conv_w": [ffn_conv_w[i] for i in range(depth)],
        "conv_b": [_row(ffn_conv_b[i]) for i in range(depth)],
        "w_down": [ffn_w_down[i].astype(_BF16) for i in range(depth)],
    }
    y_p, pool_p, hgrn_p, ffn_p = _trunk_prompt(x_prompt, wts)
    y_s, pool_s, hgrn_s, ffn_s = _trunk_sample(x_sample, past_len, state_pool, state_hgrn,
                                               state_ffn_conv, wts)
    return (y_p, y_s, pool_p, pool_s, hgrn_p, hgrn_s, ffn_p, ffn_s)
```

```python
import functools

import jax
import jax.numpy as jnp
import numpy as np
from jax import lax
from jax.experimental import pallas as pl
from jax.experimental.pallas import tpu as pltpu

D_MODEL = 1024
POOL_WINDOWS = (2, 4, 8, 16)
POOL_GROUP_DIM = D_MODEL // len(POOL_WINDOWS)
POOL_CTX = max(POOL_WINDOWS) - 1
N_HEADS = 8
HEAD_K = 128
HEAD_V = D_MODEL // N_HEADS
F_DIM = N_HEADS * HEAD_K
V_DIM = N_HEADS * HEAD_V
D_FF = 2816
CONV_W = 3
EPS = 1e-6
LOG2_E = 1.4426950408889634

SUBLANES = 8
BF16_ROWS = 16
ROW_TILE = 512
FF_CHUNK = 256
PROJ_CHUNK = 256
GLA_CHUNK = 64
GLA_LEVELS = 6
GLA_FINE_LEVELS = 3
GLA_UNROLL = 2
VMEM_LIMIT = 56 * 1024 * 1024

_F32 = jnp.float32
_BF16 = jnp.bfloat16


def _rms(x, g):
    ms = jnp.mean(x * x, axis=-1, keepdims=True)
    return x * lax.rsqrt(ms + EPS) * g


def _dot(a, b):
    return jnp.dot(a, b, preferred_element_type=_F32)


def _dot_nt(a, b):
    return lax.dot_general(a, b, (((1,), (1,)), ((), ())), preferred_element_type=_F32)


def _dot_tn(a, b):
    return lax.dot_general(a, b, (((0,), (0,)), ((), ())), preferred_element_type=_F32)


def _const_spec(shape):
    nd = len(shape)
    return pl.BlockSpec(shape, lambda *_: (0,) * nd, pipeline_mode=pl.Buffered(1))


def _params():
    return pltpu.CompilerParams(
        dimension_semantics=("arbitrary", "arbitrary"), vmem_limit_bytes=VMEM_LIMIT)


def _cast_specs(layers, grid):
    n_steps = grid[0] * grid[1]
    in_specs, out_specs, shapes = [], [], []
    for w, layer in layers:
        _, r, c = w.shape
        d = max(k for k in range(1, n_steps + 1) if r % k == 0 and (r // k) % BF16_ROWS == 0)

        def blk(b, l, d=d):
            return jnp.minimum(b * grid[1] + l, d - 1)

        in_specs.append(pl.BlockSpec(
            (None, r // d, c), lambda b, l, blk=blk, layer=layer: (layer, blk(b, l), 0)))
        out_specs.append(pl.BlockSpec((r // d, c), lambda b, l, blk=blk: (blk(b, l), 0)))
        shapes.append(jax.ShapeDtypeStruct((r, c), _BF16))
    return in_specs, out_specs, shapes


def _cast_blocks(src_refs, dst_refs):
    for src, dst in zip(src_refs, dst_refs):
        dst[...] = src[...].astype(_BF16)


def _fold_rows(a):
    rows, c = a.shape
    return a.reshape(SUBLANES, rows // SUBLANES, c).swapaxes(0, 1).reshape(rows, c)


def _unfold_rows(a):
    rows, c = a.shape
    return a.reshape(rows // SUBLANES, SUBLANES, c).swapaxes(0, 1).reshape(rows, c)


def _pool_kernel(*refs, rows, step, pos0, n_tiles, n_cast):
    x_ref, ctx_ref, gpre_ref, w_ref, scale_ref, gpost_ref = refs[:6]
    y_ref, st_ref = refs[6 + n_cast:8 + n_cast]
    hbuf = refs[-1]
    _cast_blocks(refs[6:6 + n_cast], refs[8 + n_cast:8 + 2 * n_cast])
    halo = (POOL_CTX + 1) * step
    l = pl.program_id(1)

    @pl.when(l == 0)
    def _():
        hbuf[0:step, :] = jnp.zeros((step, D_MODEL), _F32)
        hbuf[step:halo, :] = ctx_ref[0]

    x = x_ref[0]
    h = _rms(x, gpre_ref[...])
    hbuf[halo:halo + rows, :] = h

    if pos0 < POOL_CTX:
        assert step == 1
        pos = pos0 + l * rows + lax.broadcasted_iota(jnp.int32, (rows, 1), 0)
    outs = []
    for g, w in enumerate(POOL_WINDOWS):
        cs = slice(g * POOL_GROUP_DIM, (g + 1) * POOL_GROUP_DIM)
        s = hbuf[:, cs]
        span = 1
        while span < w:
            s = s + pltpu.roll(s, span * step, axis=0)
            span *= 2
        s = s[halo:halo + rows]
        if pos0 < POOL_CTX:
            cnt = jnp.minimum(pos + 1, w).astype(_F32)
        else:
            cnt = float(w)
        p = s / cnt - h[:, cs]
        outs.append(_dot(p.astype(_BF16), w_ref[g]))
    m = jnp.concatenate(outs, axis=-1) * scale_ref[...]
    y_ref[0] = x + _rms(m, gpost_ref[...])

    @pl.when(l == n_tiles - 1)
    def _():
        st_ref[0] = hbuf[rows + step:rows + halo, :]

    if n_tiles > 1:
        hbuf[0:halo, :] = hbuf[rows:rows + halo, :]


def _pool_layer(x, ctx, gpre, w_bf16, scale, gpost, *, rows, step, pos0, cast=()):
    nb, n, _ = x.shape
    n_tiles = n // rows
    halo = (POOL_CTX + 1) * step
    ctx_rows = POOL_CTX * step
    cast_in, cast_out, cast_shapes = _cast_specs(cast, (nb, n_tiles))
    kern = functools.partial(_pool_kernel, rows=rows, step=step, pos0=pos0, n_tiles=n_tiles,
                             n_cast=len(cast))
    return pl.pallas_call(
        kern,
        grid=(nb, n_tiles),
        in_specs=[
            pl.BlockSpec((1, rows, D_MODEL), lambda b, l: (b, l, 0)),
            pl.BlockSpec((1, ctx_rows, D_MODEL), lambda b, l: (b, 0, 0)),
            _const_spec((1, D_MODEL)),
            _const_spec(w_bf16.shape),
            _const_spec((1, D_MODEL)),
            _const_spec((1, D_MODEL)),
        ] + cast_in,
        out_specs=[
            pl.BlockSpec((1, rows, D_MODEL), lambda b, l: (b, l, 0)),
            pl.BlockSpec((1, ctx_rows, D_MODEL), lambda b, l: (b, 0, 0)),
        ] + cast_out,
        out_shape=[
            jax.ShapeDtypeStruct(x.shape, _F32),
            jax.ShapeDtypeStruct((nb, ctx_rows, D_MODEL), _F32),
        ] + cast_shapes,
        scratch_shapes=[pltpu.VMEM((halo + rows, D_MODEL), _F32)],
        compiler_params=_params(),
        name="pool_mixer",
    )(x, ctx, gpre, w_bf16, scale, gpost, *(w for w, _ in cast))


def _ffn_kernel(*refs, rows, step, n_tiles, fold, n_cast):
    x_ref, ctx_ref, gpre_ref, wup_ref, cw_ref, cb_ref, wdn_ref, gpost_ref = refs[:8]
    y_ref, st_ref = refs[8 + n_cast:10 + n_cast]
    carry, gbuf, vbuf, abuf = refs[-4:]
    _cast_blocks(refs[8:8 + n_cast], refs[10 + n_cast:10 + 2 * n_cast])
    shift = SUBLANES if fold else step
    halo = max(SUBLANES, (CONV_W - 1) * shift)
    taps = CONV_W - 1
    l = pl.program_id(1)

    @pl.when(l == 0)
    def _():
        carry[...] = jnp.zeros(carry.shape, _F32)
        if fold:
            for j in range(taps):
                carry[(j + 1) * SUBLANES - 1:(j + 1) * SUBLANES, :] = ctx_ref[0, j:j + 1, :]
        else:
            carry[halo - taps * step:halo, :] = ctx_ref[0]

    x = x_ref[0]
    h = _rms(x, gpre_ref[...])
    if fold:
        h = _fold_rows(h)
        sub0 = lax.broadcasted_iota(jnp.int32, (SUBLANES, 1), 0) == 0
    h = h.astype(_BF16)
    for c in range(D_FF // FF_CHUNK):
        conv = []
        for buf, off in ((gbuf, 0), (vbuf, D_FF)):
            cs = slice(off + c * FF_CHUNK, off + (c + 1) * FF_CHUNK)
            u = _dot(h, wup_ref[:, cs])
            if fold:
                for j in range(taps):
                    grp = u[rows - (taps - j) * SUBLANES:rows - (taps - j - 1) * SUBLANES, :]
                    prev = carry[(j + 1) * SUBLANES - 1:(j + 1) * SUBLANES, cs]
                    buf[j * SUBLANES:(j + 1) * SUBLANES, :] = jnp.where(
                        sub0, prev, pltpu.roll(grp, 1, axis=0))
                carry[:, cs] = u[rows - halo:rows, :]
                buf[halo:halo + rows, :] = u
            else:
                buf[0:halo, :] = carry[:, cs]
                buf[halo:halo + rows, :] = u
                carry[:, cs] = buf[rows:rows + halo, :]
            conv.append(cb_ref[:, cs]
                        + buf[halo - 2 * shift:halo - 2 * shift + rows, :] * cw_ref[0:1, cs]
                        + buf[halo - shift:halo - shift + rows, :] * cw_ref[1:2, cs]
                        + u * cw_ref[2:3, cs])
        a = jax.nn.gelu(conv[0], approximate=True) * conv[1]
        abuf[:, c * FF_CHUNK:(c + 1) * FF_CHUNK] = a.astype(_BF16)
    m = _rms(_dot(abuf[...], wdn_ref[...]), gpost_ref[...])
    if fold:
        m = _unfold_rows(m)
    y_ref[0] = x + m

    @pl.when(l == n_tiles - 1)
    def _():
        if fold:
            for j in range(taps):
                st_ref[0, j:j + 1, :] = carry[(j + 1) * SUBLANES - 1:(j + 1) * SUBLANES, :]
        else:
            st_ref[0] = carry[halo - taps * step:halo, :]


def _ffn_layer(x, ctx, gpre, wup_bf16, cw, cb, wdn_bf16, gpost, *, rows, step, fold=False,
               cast=()):
    nb, n, _ = x.shape
    n_tiles = n // rows
    cast_in, cast_out, cast_shapes = _cast_specs(cast, (nb, n_tiles))
    assert not fold or (step == 1 and rows % (SUBLANES * SUBLANES) == 0)
    halo = max(SUBLANES, (CONV_W - 1) * (SUBLANES if fold else step))
    ctx_rows = (CONV_W - 1) * step
    kern = functools.partial(_ffn_kernel, rows=rows, step=step, n_tiles=n_tiles, fold=fold,
                             n_cast=len(cast))
    return pl.pallas_call(
        kern,
        grid=(nb, n_tiles),
        in_specs=[
            pl.BlockSpec((1, rows, D_MODEL), lambda b, l: (b, l, 0)),
            pl.BlockSpec((1, ctx_rows, 2 * D_FF), lambda b, l: (b, 0, 0)),
            _const_spec((1, D_MODEL)),
            _const_spec((D_MODEL, 2 * D_FF)),
            _const_spec((CONV_W, 2 * D_FF)),
            _const_spec((1, 2 * D_FF)),
            _const_spec((D_FF, D_MODEL)),
            _const_spec((1, D_MODEL)),
        ] + cast_in,
        out_specs=[
            pl.BlockSpec((1, rows, D_MODEL), lambda b, l: (b, l, 0)),
            pl.BlockSpec((1, ctx_rows, 2 * D_FF), lambda b, l: (b, 0, 0)),
        ] + cast_out,
        out_shape=[
            jax.ShapeDtypeStruct(x.shape, _F32),
            jax.ShapeDtypeStruct((nb, ctx_rows, 2 * D_FF), _F32),
        ] + cast_shapes,
        scratch_shapes=[
            pltpu.VMEM((halo, 2 * D_FF), _F32),
            pltpu.VMEM((halo + rows, FF_CHUNK), _F32),
            pltpu.VMEM((halo + rows, FF_CHUNK), _F32),
            pltpu.VMEM((rows, D_FF), _BF16),
        ],
        compiler_params=_params(),
        name="conv_ffn",
    )(x, ctx, gpre, wup_bf16, cw, cb, wdn_bf16, gpost, *(w for w, _ in cast))


def _silu(x):
    return x * (0.5 * jnp.tanh(0.5 * x) + 0.5)


def _hgrn_gates(h, win_ref, lbl_ref, q_ref, k_ref, lf_ref, v_ref, gs_ref):
    l0 = lbl_ref[0:1, :]
    l1 = lbl_ref[1:2, :]
    mx = jnp.maximum(l0, l1)
    e0 = jnp.exp(l0 - mx)
    e1 = jnp.exp(l1 - mx)
    lb_all = e1 / (e0 + e1)

    assert F_DIM == V_DIM
    for c in range(F_DIM // PROJ_CHUNK):
        cs = slice(c * PROJ_CHUNK, (c + 1) * PROJ_CHUNK)
        fr = _dot(h, win_ref[:, F_DIM + c * PROJ_CHUNK:F_DIM + (c + 1) * PROJ_CHUNK])
        off = 2 * F_DIM + c * PROJ_CHUNK
        v_ref[:, cs] = _dot(h, win_ref[:, off:off + PROJ_CHUNK]).astype(v_ref.dtype)
        qr = _dot(h, win_ref[:, cs])
        gr = _dot(h, win_ref[:, off + V_DIM:off + V_DIM + PROJ_CHUNK])
        e = jnp.exp(-jnp.abs(fr))
        big = 1.0 / (1.0 + e)
        small = e * big
        pos = fr >= 0.0
        lb = lb_all[:, cs]
        lf_ref[:, cs] = jnp.log(lb + (1.0 - lb) * jnp.where(pos, big, small))
        k_ref[:, cs] = (1.0 - lb) * jnp.where(pos, small, big)
        q_ref[:, cs] = _silu(qr) * (HEAD_K ** -0.5)
        gs_ref[:, cs] = _silu(gr)


def _hproj_kernel(x_ref, gpre_ref, win_ref, lbl_ref, q_ref, k_ref, lf_ref, v_ref, gs_ref):
    h = _rms(x_ref[...], gpre_ref[...]).astype(_BF16)
    _hgrn_gates(h, win_ref, lbl_ref, q_ref, k_ref, lf_ref, v_ref, gs_ref)


def _hgrn_proj(x2d, gpre, win_bf16, lb_logits, *, rows):
    n = x2d.shape[0]
    row_spec = pl.BlockSpec((rows, D_MODEL), lambda i: (i, 0))
    out = jax.ShapeDtypeStruct((n, D_MODEL), _F32)
    return pl.pallas_call(
        _hproj_kernel,
        grid=(n // rows,),
        in_specs=[row_spec, _const_spec((1, D_MODEL)), _const_spec(win_bf16.shape),
                  _const_spec(lb_logits.shape)],
        out_specs=[row_spec] * 5,
        out_shape=[out] * 5,
        compiler_params=pltpu.CompilerParams(
            dimension_semantics=("arbitrary",), vmem_limit_bytes=VMEM_LIMIT),
        name="hgrn_proj",
    )(x2d, gpre, win_bf16, lb_logits)


def _hout_kernel(x_ref, o_ref, gs_ref, gn_ref, wout_ref, gpost_ref, y_ref, zbuf):
    for hd in range(N_HEADS):
        hs = slice(hd * HEAD_V, (hd + 1) * HEAD_V)
        zbuf[:, hs] = (_rms(o_ref[:, hs], gn_ref[...]) * gs_ref[:, hs]).astype(_BF16)
    m = _dot(zbuf[...], wout_ref[...])
    y_ref[...] = x_ref[...] + _rms(m, gpost_ref[...])


def _hgrn_out(x2d, o2d, gs2d, gnorm, wout_bf16, gpost, *, rows):
    n = x2d.shape[0]
    row_spec = pl.BlockSpec((rows, D_MODEL), lambda i: (i, 0))
    return pl.pallas_call(
        _hout_kernel,
        grid=(n // rows,),
        in_specs=[row_spec, row_spec, row_spec, _const_spec((1, HEAD_V)),
                  _const_spec(wout_bf16.shape), _const_spec((1, D_MODEL))],
        out_specs=row_spec,
        out_shape=jax.ShapeDtypeStruct((n, D_MODEL), _F32),
        scratch_shapes=[pltpu.VMEM((rows, V_DIM), _BF16)],
        compiler_params=pltpu.CompilerParams(
            dimension_semantics=("arbitrary",), vmem_limit_bytes=VMEM_LIMIT),
        name="hgrn_out",
    )(x2d, o2d, gs2d, gnorm, wout_bf16, gpost)


def _gla_exponent_matrix():
    c = GLA_CHUNK
    t = np.arange(c)[:, None]
    w = np.arange(c)[None, :]
    blocks = [(w <= t)]
    for lev in range(1, GLA_FINE_LEVELS + 1):
        n = 1 << lev
        mid = (t // n) * n + n // 2 - 1
        later = (t > mid) & (w > mid) & (w <= t)
        earlier = (t <= mid) & (w > t) & (w <= mid)
        blocks.append(later | earlier)
    return np.concatenate(blocks, axis=0).astype(np.float32)


def _hgrn_prompt_kernel(x_ref, s0_ref, gpre_ref, win_ref, lbl_ref, mall_ref, gn_ref, wout_ref,
                        gpost_ref, y_ref, sn_ref, st, q_buf, k_buf, lf_buf, v_buf, gs_buf, z_buf,
                        *, rows, n_tiles):
    c = GLA_CHUNK
    l = pl.program_id(1)

    @pl.when(l == 0)
    def _():
        for hd in range(N_HEADS):
            st[hd] = s0_ref[0, hd].T

    x = x_ref[0]
    _hgrn_gates(_rms(x, gpre_ref[...]).astype(_BF16), win_ref, lbl_ref,
                q_buf, k_buf, lf_buf, v_buf, gs_buf)

    ti = lax.broadcasted_iota(jnp.int32, (c, c), 0)
    si = lax.broadcasted_iota(jnp.int32, (c, c), 1)
    ri = lax.broadcasted_iota(jnp.int32, (c, 1), 0)
    diag = ti == si
    lvl_mask, lvl_role = [], []
    for lev in range(1, GLA_LEVELS + 1):
        same = (ti >> lev) == (si >> lev)
        t_late = ((ti >> (lev - 1)) & 1) == 1
        s_early = ((si >> (lev - 1)) & 1) == 0
        lvl_mask.append(same & t_late & s_early)
        lvl_role.append(((ri >> (lev - 1)) & 1) == 1)

    def chunk(ci, carry):
        rs = pl.ds(pl.multiple_of(ci * c, c), c)
        lf = lf_buf[rs, :] * LOG2_E
        lf_hi = lf.astype(_BF16)
        r1 = lf - lf_hi.astype(_F32)
        lf_mid = r1.astype(_BF16)
        lf_lo = (r1 - lf_mid.astype(_F32)).astype(_BF16)
        cum = _dot(mall_ref[...], jnp.concatenate([lf_hi, lf_mid, lf_lo], axis=0))
        b = cum[0:c]
        eb = jnp.exp2(b)
        eb_rest = jnp.exp2(b[c - 1:c] - b)
        eb_last = eb[c - 1:c]
        q = q_buf[rs, :]
        k = k_buf[rs, :]
        vb = v_buf[rs, :]
        qd = (q * eb).astype(_BF16)
        kd = (k * eb_rest).astype(_BF16)
        lvl_x = []
        for lev in range(1, GLA_LEVELS + 1):
            n = 1 << lev
            if lev <= GLA_FINE_LEVELS:
                src = jnp.where(lvl_role[lev - 1], q, k)
                expo = cum[lev * c:(lev + 1) * c]
            else:
                src, expo = [], []
                for r0 in range(0, c, n):
                    mid = r0 + n // 2 - 1
                    src += [k[r0:mid + 1], q[mid + 1:r0 + n]]
                    expo += [b[mid:mid + 1] - b[r0:mid + 1], b[mid + 1:r0 + n] - b[mid:mid + 1]]
                src = jnp.concatenate(src, axis=0)
                expo = jnp.concatenate(expo, axis=0)
            lvl_x.append((src * jnp.exp2(expo)).astype(_BF16))
        qk = q * k
        heads = [slice(hd * HEAD_K, (hd + 1) * HEAD_K) for hd in range(N_HEADS)]
        scores = []
        for hs in heads:
            a = jnp.where(diag, jnp.sum(qk[:, hs], axis=-1, keepdims=True), 0.0)
            for lev in range(GLA_LEVELS):
                xl = lvl_x[lev][:, hs]
                a = jnp.where(lvl_mask[lev], _dot_nt(xl, xl), a)
            scores.append(a.astype(_BF16))
        s_old = [st[hd] for hd in range(N_HEADS)]
        for hd, hs in enumerate(heads):
            st[hd] = s_old[hd] * eb_last[:, hs] + _dot_tn(vb[:, hs], kd[:, hs])
        for hd, hs in enumerate(heads):
            o = _dot_nt(qd[:, hs], s_old[hd].astype(_BF16)) + _dot(scores[hd], vb[:, hs])
            z_buf[rs, hs] = (_rms(o, gn_ref[...]) * gs_buf[rs, hs]).astype(_BF16)
        return carry

    lax.fori_loop(0, rows // c, chunk, 0, unroll=GLA_UNROLL)

    m = _dot(z_buf[...], wout_ref[...])
    y_ref[0] = x + _rms(m, gpost_ref[...])

    @pl.when(l == n_tiles - 1)
    def _():
        for hd in range(N_HEADS):
            sn_ref[0, hd] = st[hd].T


def _hgrn_prompt(x, s0, gpre, win_bf16, lb_logits, gnorm, wout_bf16, gpost, *, rows):
    nb, n, _ = x.shape
    n_tiles = n // rows
    mall = jnp.asarray(np.tile(_gla_exponent_matrix(), (1, 3)), dtype=_BF16)
    row_spec = pl.BlockSpec((1, rows, D_MODEL), lambda b, l: (b, l, 0))
    st_spec = pl.BlockSpec((1, N_HEADS, HEAD_K, HEAD_V), lambda b, l: (b, 0, 0, 0))
    kern = functools.partial(_hgrn_prompt_kernel, rows=rows, n_tiles=n_tiles)
    return pl.pallas_call(
        kern,
        grid=(nb, n_tiles),
        in_specs=[row_spec, st_spec, _const_spec((1, D_MODEL)), _const_spec(win_bf16.shape),
                  _const_spec(lb_logits.shape), _const_spec(mall.shape), _const_spec((1, HEAD_V)),
                  _const_spec(wout_bf16.shape), _const_spec((1, D_MODEL))],
        out_specs=[row_spec, st_spec],
        out_shape=[jax.ShapeDtypeStruct(x.shape, _F32), jax.ShapeDtypeStruct(s0.shape, _F32)],
        scratch_shapes=[
            pltpu.VMEM((N_HEADS, HEAD_V, HEAD_K), _F32),
            pltpu.VMEM((rows, F_DIM), _F32),
            pltpu.VMEM((rows, F_DIM), _F32),
            pltpu.VMEM((rows, F_DIM), _F32),
            pltpu.VMEM((rows, V_DIM), _BF16),
            pltpu.VMEM((rows, V_DIM), _F32),
            pltpu.VMEM((rows, V_DIM), _BF16),
        ],
        compiler_params=_params(),
        name="hgrn_prompt",
    )(x, s0, gpre, win_bf16, lb_logits, mall, gnorm, wout_bf16, gpost)


def _gla_sample_kernel(q_ref, k_ref, lf_ref, v_ref, s0_ref, o_ref, sn_ref, *, steps):
    heads = [slice(hd * HEAD_K, (hd + 1) * HEAD_K) for hd in range(N_HEADS)]
    q = [q_ref[t] for t in range(steps)]
    k = [k_ref[t] for t in range(steps)]
    v = [v_ref[t] for t in range(steps)]
    b = [lf_ref[0]]
    for t in range(1, steps):
        b.append(b[-1] + lf_ref[t])
    b_last = b[-1]
    eb_last = jnp.exp(b_last)
    qd = [q[t] * jnp.exp(b[t]) for t in range(steps)]
    kd = [k[s] * jnp.exp(b_last - b[s]) for s in range(steps)]

    o_intra = []
    for t in range(steps):
        acc = None
        for s in range(t + 1):
            p = q[t] * k[s] if s == t else q[t] * k[s] * jnp.exp(b[t] - b[s])
            part = jnp.concatenate(
                [jnp.sum(p[:, hs], axis=-1, keepdims=True) * v[s][:, hs] for hs in heads], axis=-1)
            acc = part if acc is None else acc + part
        o_intra.append(acc)

    zero = jnp.zeros_like(b_last)
    rows_pad = [zero] * (SUBLANES - steps)
    e_hi = eb_last.astype(_BF16).astype(_F32)
    e_r = eb_last - e_hi
    e_mid = e_r.astype(_BF16).astype(_F32)
    e_lo = e_r - e_mid
    assert steps + 3 <= SUBLANES
    qd_b = jnp.stack(qd + rows_pad).swapaxes(0, 1).astype(_BF16)
    v_b = jnp.stack(v + rows_pad).swapaxes(0, 1)
    kd_b = jnp.stack(kd + [e_hi, e_mid, e_lo] + rows_pad[3:]).swapaxes(0, 1).astype(_BF16)
    ri = lax.broadcasted_iota(jnp.int32, (SUBLANES, HEAD_V), 0)
    ones_rows = ((ri >= steps) & (ri < steps + 3)).astype(_F32)

    o_inter = []
    for bi in range(SUBLANES):
        outs = []
        for hd, hs in enumerate(heads):
            s0 = s0_ref[bi, hd]
            outs.append(_dot(qd_b[bi][:, hs], s0.astype(_BF16)))
            rhs = jnp.concatenate([v_b[bi][:, hs], ones_rows], axis=-1).astype(_BF16)
            both = _dot_tn(kd_b[bi][:, hs], rhs)
            sn_ref[bi, hd] = both[:, HEAD_V:] * s0 + both[:, :HEAD_V]
        o_inter.append(jnp.concatenate(outs, axis=-1))
    o_inter = jnp.stack(o_inter).swapaxes(0, 1)
    for t in range(steps):
        o_ref[t] = o_intra[t] + o_inter[t]


def _gla_sample(q, k, lf, v, s0):
    steps, nb, _ = q.shape
    row_spec = pl.BlockSpec((steps, SUBLANES, D_MODEL), lambda i: (0, i, 0))
    st_spec = pl.BlockSpec((SUBLANES, N_HEADS, HEAD_K, HEAD_V), lambda i: (i, 0, 0, 0))
    kern = functools.partial(_gla_sample_kernel, steps=steps)
    return pl.pallas_call(
        kern,
        grid=(nb // SUBLANES,),
        in_specs=[row_spec, row_spec, row_spec, row_spec, st_spec],
        out_specs=[row_spec, st_spec],
        out_shape=[jax.ShapeDtypeStruct(q.shape, _F32), jax.ShapeDtypeStruct(s0.shape, _F32)],
        compiler_params=pltpu.CompilerParams(
            dimension_semantics=("arbitrary",), vmem_limit_bytes=VMEM_LIMIT),
        name="gla_sample",
    )(q, k, lf, v, s0)


def _row(v):
    return v.reshape(1, -1)


def _trunk_prompt(x, wts):
    nb, n, d = x.shape
    pool_ctx = jnp.zeros((nb, POOL_CTX, d), _F32)
    ffn_ctx = jnp.zeros((nb, CONV_W - 1, 2 * D_FF), _F32)
    s0 = jnp.zeros((nb, N_HEADS, HEAD_K, HEAD_V), _F32)

    f32 = wts["f32"]
    x, pool_st, w_up0, w_down0 = _pool_layer(
        x, pool_ctx, wts["mix_pre"][0], wts["pool_w"], wts["pool_scale"], wts["mix_post"][0],
        rows=ROW_TILE, step=1, pos0=0, cast=[(f32["w_up"], 0), (f32["w_down"], 0)])
    x, ffn_st0, w_in, w_out, w_up1, w_down1 = _ffn_layer(
        x, ffn_ctx, wts["ffn_pre"][0], w_up0, wts["conv_w"][0], wts["conv_b"][0], w_down0,
        wts["ffn_post"][0], rows=ROW_TILE, step=1, fold=True,
        cast=[(f32["w_in"], 0), (f32["w_out"], 0), (f32["w_up"], 1), (f32["w_down"], 1)])
    wts.update(w_in=w_in, w_out=w_out, w_up=[w_up0, w_up1], w_down=[w_down0, w_down1])
    x, s_new = _hgrn_prompt(x, s0, wts["mix_pre"][1], wts["w_in"], wts["lb_logits"], wts["gnorm"],
                            wts["w_out"], wts["mix_post"][1], rows=ROW_TILE)
    x, ffn_st1 = _ffn_layer(x, ffn_ctx, wts["ffn_pre"][1], wts["w_up"][1],
                            wts["conv_w"][1], wts["conv_b"][1], wts["w_down"][1],
                            wts["ffn_post"][1], rows=ROW_TILE, step=1, fold=True)
    return x, pool_st[None], s_new[None], jnp.stack([ffn_st0, ffn_st1])


def _trunk_sample(x, pos0, state_pool, state_hgrn, state_ffn, wts):
    nb, steps, d = x.shape
    n = nb * steps

    def to_time_major(a):
        return a.transpose(1, 0, 2).reshape(1, a.shape[1] * nb, a.shape[2])

    def to_batch_major(a, t):
        return a.reshape(t, nb, a.shape[-1]).transpose(1, 0, 2)

    xt = to_time_major(x)
    xt, pool_st = _pool_layer(xt, to_time_major(state_pool[0]), wts["mix_pre"][0], wts["pool_w"],
                              wts["pool_scale"], wts["mix_post"][0], rows=n, step=nb, pos0=pos0)
    xt, ffn_st0 = _ffn_layer(xt, to_time_major(state_ffn[0]), wts["ffn_pre"][0], wts["w_up"][0],
                             wts["conv_w"][0], wts["conv_b"][0], wts["w_down"][0],
                             wts["ffn_post"][0], rows=n, step=nb)
    x2d = xt.reshape(n, d)
    proj = _hgrn_proj(x2d, wts["mix_pre"][1], wts["w_in"], wts["lb_logits"], rows=n)

    q, k, lf, v = (a.reshape(steps, nb, d) for a in proj[:4])
    o, s_new = _gla_sample(q, k, lf, v, state_hgrn[0])
    x2d = _hgrn_out(x2d, o.reshape(n, d), proj[4], wts["gnorm"], wts["w_out"],
                    wts["mix_post"][1], rows=n)
    xt, ffn_st1 = _ffn_layer(x2d.reshape(1, n, d), to_time_major(state_ffn[1]), wts["ffn_pre"][1],
                             wts["w_up"][1], wts["conv_w"][1], wts["conv_b"][1],
                             wts["w_down"][1], wts["ffn_post"][1], rows=n, step=nb)
    y = to_batch_major(xt, steps)
    new_pool = to_batch_major(pool_st, POOL_CTX)[None]
    new_ffn = jnp.stack([to_batch_major(ffn_st0, CONV_W - 1), to_batch_major(ffn_st1, CONV_W - 1)])
    return y, new_pool, s_new[None], new_ffn


def kernel(x_prompt, x_sample, state_pool, state_hgrn, state_ffn_conv, norm_mix_pre, norm_mix_post,
           norm_ffn_pre, norm_ffn_post, pool_w, pool_scale, hgrn_w_in, hgrn_lb_logits, hgrn_gnorm,
           hgrn_w_out, ffn_w_up, ffn_conv_w, ffn_conv_b, ffn_w_down):
    depth = ffn_w_up.shape[0]
    assert depth == 2 and pool_w.shape[0] == 1 and hgrn_w_in.shape[0] == 1
    past_len = 16384
    wts = {
        "mix_pre": [_row(norm_mix_pre[i]) for i in range(depth)],
        "mix_post": [_row(norm_mix_post[i]) for i in range(depth)],
        "ffn_pre": [_row(norm_ffn_pre[i]) for i in range(depth)],
        "ffn_post": [_row(norm_ffn_post[i]) for i in range(depth)],
        "pool_w": pool_w[0].astype(_BF16),
        "pool_scale": _row(pool_scale[0]),
        "lb_logits": hgrn_lb_logits,
        "gnorm": _row(hgrn_gnorm[0]),
        "conv_w": [ffn_conv_w[i] for i in range(depth)],
        "conv_b": [_row(ffn_conv_b[i]) for i in range(depth)],
        "f32": {"w_in": hgrn_w_in, "w_out": hgrn_w_out, "w_up": ffn_w_up, "w_down": ffn_w_down},
    }
    y_p, pool_p, hgrn_p, ffn_p = _trunk_prompt(x_prompt, wts)
    y_s, pool_s, hgrn_s, ffn_s = _trunk_sample(x_sample, past_len, state_pool, state_hgrn,
                                               state_ffn_conv, wts)
    return (y_p, y_s, pool_p, pool_s, hgrn_p, hgrn_s, ffn_p, ffn_s)
```

```python
import functools

import jax
import jax.numpy as jnp
import numpy as np
from jax import lax
from jax.experimental import pallas as pl
from jax.experimental.pallas import tpu as pltpu

D_MODEL = 1024
POOL_WINDOWS = (2, 4, 8, 16)
POOL_GROUP_DIM = D_MODEL // len(POOL_WINDOWS)
POOL_CTX = max(POOL_WINDOWS) - 1
N_HEADS = 8
HEAD_K = 128
HEAD_V = D_MODEL // N_HEADS
F_DIM = N_HEADS * HEAD_K
V_DIM = N_HEADS * HEAD_V
D_FF = 2816
CONV_W = 3
EPS = 1e-6
LOG2_E = 1.4426950408889634

SUBLANES = 8
BF16_ROWS = 16
ROW_TILE = 512
FF_CHUNK = 256
PROJ_CHUNK = 256
GLA_CHUNK = 64
GLA_LEVELS = 6
GLA_FINE_LEVELS = 3
GLA_UNROLL = 8
VMEM_LIMIT = 56 * 1024 * 1024

_F32 = jnp.float32
_BF16 = jnp.bfloat16


def _rms(x, g):
    ms = jnp.mean(x * x, axis=-1, keepdims=True)
    return x * lax.rsqrt(ms + EPS) * g


def _dot(a, b):
    return jnp.dot(a, b, preferred_element_type=_F32)


def _dot_nt(a, b):
    return lax.dot_general(a, b, (((1,), (1,)), ((), ())), preferred_element_type=_F32)


def _dot_tn(a, b):
    return lax.dot_general(a, b, (((0,), (0,)), ((), ())), preferred_element_type=_F32)


def _const_spec(shape):
    nd = len(shape)
    return pl.BlockSpec(shape, lambda *_: (0,) * nd, pipeline_mode=pl.Buffered(1))


def _params():
    return pltpu.CompilerParams(
        dimension_semantics=("arbitrary", "arbitrary"), vmem_limit_bytes=VMEM_LIMIT)


def _cast_specs(layers, grid):
    n_steps = grid[0] * grid[1]
    in_specs, out_specs, shapes = [], [], []
    for w, layer in layers:
        _, r, c = w.shape
        d = max(k for k in range(1, n_steps + 1) if r % k == 0 and (r // k) % BF16_ROWS == 0)

        def blk(b, l, d=d):
            return jnp.minimum(b * grid[1] + l, d - 1)

        in_specs.append(pl.BlockSpec(
            (None, r // d, c), lambda b, l, blk=blk, layer=layer: (layer, blk(b, l), 0)))
        out_specs.append(pl.BlockSpec((r // d, c), lambda b, l, blk=blk: (blk(b, l), 0)))
        shapes.append(jax.ShapeDtypeStruct((r, c), _BF16))
    return in_specs, out_specs, shapes


def _cast_blocks(src_refs, dst_refs):
    for src, dst in zip(src_refs, dst_refs):
        dst[...] = src[...].astype(_BF16)


def _fold_rows(a):
    rows, c = a.shape
    return a.reshape(SUBLANES, rows // SUBLANES, c).swapaxes(0, 1).reshape(rows, c)


def _unfold_rows(a):
    rows, c = a.shape
    return a.reshape(rows // SUBLANES, SUBLANES, c).swapaxes(0, 1).reshape(rows, c)


def _pool_kernel(*refs, rows, step, pos0, n_tiles, n_cast):
    x_ref, ctx_ref, gpre_ref, w_ref, scale_ref, gpost_ref = refs[:6]
    y_ref, st_ref = refs[6 + n_cast:8 + n_cast]
    hbuf = refs[-1]
    _cast_blocks(refs[6:6 + n_cast], refs[8 + n_cast:8 + 2 * n_cast])
    halo = (POOL_CTX + 1) * step
    l = pl.program_id(1)

    @pl.when(l == 0)
    def _():
        hbuf[0:step, :] = jnp.zeros((step, D_MODEL), _F32)
        hbuf[step:halo, :] = ctx_ref[0]

    x = x_ref[0]
    h = _rms(x, gpre_ref[...])
    hbuf[halo:halo + rows, :] = h

    if pos0 < POOL_CTX:
        assert step == 1
        pos = pos0 + l * rows + lax.broadcasted_iota(jnp.int32, (rows, 1), 0)
    outs = []
    for g, w in enumerate(POOL_WINDOWS):
        cs = slice(g * POOL_GROUP_DIM, (g + 1) * POOL_GROUP_DIM)
        s = hbuf[:, cs]
        span = 1
        while span < w:
            s = s + pltpu.roll(s, span * step, axis=0)
            span *= 2
        s = s[halo:halo + rows]
        if pos0 < POOL_CTX:
            cnt = jnp.minimum(pos + 1, w).astype(_F32)
        else:
            cnt = float(w)
        p = s / cnt - h[:, cs]
        outs.append(_dot(p.astype(_BF16), w_ref[g]))
    m = jnp.concatenate(outs, axis=-1) * scale_ref[...]
    y_ref[0] = x + _rms(m, gpost_ref[...])

    @pl.when(l == n_tiles - 1)
    def _():
        st_ref[0] = hbuf[rows + step:rows + halo, :]

    if n_tiles > 1:
        hbuf[0:halo, :] = hbuf[rows:rows + halo, :]


def _pool_layer(x, ctx, gpre, w_bf16, scale, gpost, *, rows, step, pos0, cast=()):
    nb, n, _ = x.shape
    n_tiles = n // rows
    halo = (POOL_CTX + 1) * step
    ctx_rows = POOL_CTX * step
    cast_in, cast_out, cast_shapes = _cast_specs(cast, (nb, n_tiles))
    kern = functools.partial(_pool_kernel, rows=rows, step=step, pos0=pos0, n_tiles=n_tiles,
                             n_cast=len(cast))
    return pl.pallas_call(
        kern,
        grid=(nb, n_tiles),
        in_specs=[
            pl.BlockSpec((1, rows, D_MODEL), lambda b, l: (b, l, 0)),
            pl.BlockSpec((1, ctx_rows, D_MODEL), lambda b, l: (b, 0, 0)),
            _const_spec((1, D_MODEL)),
            _const_spec(w_bf16.shape),
            _const_spec((1, D_MODEL)),
            _const_spec((1, D_MODEL)),
        ] + cast_in,
        out_specs=[
            pl.BlockSpec((1, rows, D_MODEL), lambda b, l: (b, l, 0)),
            pl.BlockSpec((1, ctx_rows, D_MODEL), lambda b, l: (b, 0, 0)),
        ] + cast_out,
        out_shape=[
            jax.ShapeDtypeStruct(x.shape, _F32),
            jax.ShapeDtypeStruct((nb, ctx_rows, D_MODEL), _F32),
        ] + cast_shapes,
        scratch_shapes=[pltpu.VMEM((halo + rows, D_MODEL), _F32)],
        compiler_params=_params(),
        name="pool_mixer",
    )(x, ctx, gpre, w_bf16, scale, gpost, *(w for w, _ in cast))


def _ffn_kernel(*refs, rows, step, n_tiles, fold, n_cast):
    x_ref, ctx_ref, gpre_ref, wup_ref, cw_ref, cb_ref, wdn_ref, gpost_ref = refs[:8]
    y_ref, st_ref = refs[8 + n_cast:10 + n_cast]
    carry, gbuf, vbuf, abuf = refs[-4:]
    _cast_blocks(refs[8:8 + n_cast], refs[10 + n_cast:10 + 2 * n_cast])
    shift = SUBLANES if fold else step
    halo = max(SUBLANES, (CONV_W - 1) * shift)
    taps = CONV_W - 1
    l = pl.program_id(1)

    @pl.when(l == 0)
    def _():
        carry[...] = jnp.zeros(carry.shape, _F32)
        if fold:
            for j in range(taps):
                carry[(j + 1) * SUBLANES - 1:(j + 1) * SUBLANES, :] = ctx_ref[0, j:j + 1, :]
        else:
            carry[halo - taps * step:halo, :] = ctx_ref[0]

    x = x_ref[0]
    h = _rms(x, gpre_ref[...])
    if fold:
        h = _fold_rows(h)
        sub0 = lax.broadcasted_iota(jnp.int32, (SUBLANES, 1), 0) == 0
    h = h.astype(_BF16)
    for c in range(D_FF // FF_CHUNK):
        conv = []
        for buf, off in ((gbuf, 0), (vbuf, D_FF)):
            cs = slice(off + c * FF_CHUNK, off + (c + 1) * FF_CHUNK)
            u = _dot(h, wup_ref[:, cs])
            if fold:
                for j in range(taps):
                    grp = u[rows - (taps - j) * SUBLANES:rows - (taps - j - 1) * SUBLANES, :]
                    prev = carry[(j + 1) * SUBLANES - 1:(j + 1) * SUBLANES, cs]
                    buf[j * SUBLANES:(j + 1) * SUBLANES, :] = jnp.where(
                        sub0, prev, pltpu.roll(grp, 1, axis=0))
                carry[:, cs] = u[rows - halo:rows, :]
                buf[halo:halo + rows, :] = u
            else:
                buf[0:halo, :] = carry[:, cs]
                buf[halo:halo + rows, :] = u
                carry[:, cs] = buf[rows:rows + halo, :]
            conv.append(cb_ref[:, cs]
                        + buf[halo - 2 * shift:halo - 2 * shift + rows, :] * cw_ref[0:1, cs]
                        + buf[halo - shift:halo - shift + rows, :] * cw_ref[1:2, cs]
                        + u * cw_ref[2:3, cs])
        a = jax.nn.gelu(conv[0], approximate=True) * conv[1]
        abuf[:, c * FF_CHUNK:(c + 1) * FF_CHUNK] = a.astype(_BF16)
    m = _rms(_dot(abuf[...], wdn_ref[...]), gpost_ref[...])
    if fold:
        m = _unfold_rows(m)
    y_ref[0] = x + m

    @pl.when(l == n_tiles - 1)
    def _():
        if fold:
            for j in range(taps):
                st_ref[0, j:j + 1, :] = carry[(j + 1) * SUBLANES - 1:(j + 1) * SUBLANES, :]
        else:
            st_ref[0] = carry[halo - taps * step:halo, :]


def _ffn_layer(x, ctx, gpre, wup_bf16, cw, cb, wdn_bf16, gpost, *, rows, step, fold=False,
               cast=()):
    nb, n, _ = x.shape
    n_tiles = n // rows
    cast_in, cast_out, cast_shapes = _cast_specs(cast, (nb, n_tiles))
    assert not fold or (step == 1 and rows % (SUBLANES * SUBLANES) == 0)
    halo = max(SUBLANES, (CONV_W - 1) * (SUBLANES if fold else step))
    ctx_rows = (CONV_W - 1) * step
    kern = functools.partial(_ffn_kernel, rows=rows, step=step, n_tiles=n_tiles, fold=fold,
                             n_cast=len(cast))
    return pl.pallas_call(
        kern,
        grid=(nb, n_tiles),
        in_specs=[
            pl.BlockSpec((1, rows, D_MODEL), lambda b, l: (b, l, 0)),
            pl.BlockSpec((1, ctx_rows, 2 * D_FF), lambda b, l: (b, 0, 0)),
            _const_spec((1, D_MODEL)),
            _const_spec((D_MODEL, 2 * D_FF)),
            _const_spec((CONV_W, 2 * D_FF)),
            _const_spec((1, 2 * D_FF)),
            _const_spec((D_FF, D_MODEL)),
            _const_spec((1, D_MODEL)),
        ] + cast_in,
        out_specs=[
            pl.BlockSpec((1, rows, D_MODEL), lambda b, l: (b, l, 0)),
            pl.BlockSpec((1, ctx_rows, 2 * D_FF), lambda b, l: (b, 0, 0)),
        ] + cast_out,
        out_shape=[
            jax.ShapeDtypeStruct(x.shape, _F32),
            jax.ShapeDtypeStruct((nb, ctx_rows, 2 * D_FF), _F32),
        ] + cast_shapes,
        scratch_shapes=[
            pltpu.VMEM((halo, 2 * D_FF), _F32),
            pltpu.VMEM((halo + rows, FF_CHUNK), _F32),
            pltpu.VMEM((halo + rows, FF_CHUNK), _F32),
            pltpu.VMEM((rows, D_FF), _BF16),
        ],
        compiler_params=_params(),
        name="conv_ffn",
    )(x, ctx, gpre, wup_bf16, cw, cb, wdn_bf16, gpost, *(w for w, _ in cast))


def _silu(x):
    return x * (0.5 * jnp.tanh(0.5 * x) + 0.5)


def _hgrn_gates(h, win_ref, lbl_ref, q_ref, k_ref, lf_ref, v_ref, gs_ref):
    l0 = lbl_ref[0:1, :]
    l1 = lbl_ref[1:2, :]
    mx = jnp.maximum(l0, l1)
    e0 = jnp.exp(l0 - mx)
    e1 = jnp.exp(l1 - mx)
    lb_all = e1 / (e0 + e1)

    assert F_DIM == V_DIM
    for c in range(F_DIM // PROJ_CHUNK):
        cs = slice(c * PROJ_CHUNK, (c + 1) * PROJ_CHUNK)
        fr = _dot(h, win_ref[:, F_DIM + c * PROJ_CHUNK:F_DIM + (c + 1) * PROJ_CHUNK])
        off = 2 * F_DIM + c * PROJ_CHUNK
        v_ref[:, cs] = _dot(h, win_ref[:, off:off + PROJ_CHUNK]).astype(v_ref.dtype)
        qr = _dot(h, win_ref[:, cs])
        gr = _dot(h, win_ref[:, off + V_DIM:off + V_DIM + PROJ_CHUNK])
        e = jnp.exp(-jnp.abs(fr))
        big = 1.0 / (1.0 + e)
        small = e * big
        pos = fr >= 0.0
        lb = lb_all[:, cs]
        lf_ref[:, cs] = jnp.log(lb + (1.0 - lb) * jnp.where(pos, big, small))
        k_ref[:, cs] = (1.0 - lb) * jnp.where(pos, small, big)
        q_ref[:, cs] = _silu(qr) * (HEAD_K ** -0.5)
        gs_ref[:, cs] = _silu(gr)


def _hproj_kernel(x_ref, gpre_ref, win_ref, lbl_ref, q_ref, k_ref, lf_ref, v_ref, gs_ref):
    h = _rms(x_ref[...], gpre_ref[...]).astype(_BF16)
    _hgrn_gates(h, win_ref, lbl_ref, q_ref, k_ref, lf_ref, v_ref, gs_ref)


def _hgrn_proj(x2d, gpre, win_bf16, lb_logits, *, rows):
    n = x2d.shape[0]
    row_spec = pl.BlockSpec((rows, D_MODEL), lambda i: (i, 0))
    out = jax.ShapeDtypeStruct((n, D_MODEL), _F32)
    return pl.pallas_call(
        _hproj_kernel,
        grid=(n // rows,),
        in_specs=[row_spec, _const_spec((1, D_MODEL)), _const_spec(win_bf16.shape),
                  _const_spec(lb_logits.shape)],
        out_specs=[row_spec] * 5,
        out_shape=[out] * 5,
        compiler_params=pltpu.CompilerParams(
            dimension_semantics=("arbitrary",), vmem_limit_bytes=VMEM_LIMIT),
        name="hgrn_proj",
    )(x2d, gpre, win_bf16, lb_logits)


def _hout_kernel(x_ref, o_ref, gs_ref, gn_ref, wout_ref, gpost_ref, y_ref, zbuf):
    for hd in range(N_HEADS):
        hs = slice(hd * HEAD_V, (hd + 1) * HEAD_V)
        zbuf[:, hs] = (_rms(o_ref[:, hs], gn_ref[...]) * gs_ref[:, hs]).astype(_BF16)
    m = _dot(zbuf[...], wout_ref[...])
    y_ref[...] = x_ref[...] + _rms(m, gpost_ref[...])


def _hgrn_out(x2d, o2d, gs2d, gnorm, wout_bf16, gpost, *, rows):
    n = x2d.shape[0]
    row_spec = pl.BlockSpec((rows, D_MODEL), lambda i: (i, 0))
    return pl.pallas_call(
        _hout_kernel,
        grid=(n // rows,),
        in_specs=[row_spec, row_spec, row_spec, _const_spec((1, HEAD_V)),
                  _const_spec(wout_bf16.shape), _const_spec((1, D_MODEL))],
        out_specs=row_spec,
        out_shape=jax.ShapeDtypeStruct((n, D_MODEL), _F32),
        scratch_shapes=[pltpu.VMEM((rows, V_DIM), _BF16)],
        compiler_params=pltpu.CompilerParams(
            dimension_semantics=("arbitrary",), vmem_limit_bytes=VMEM_LIMIT),
        name="hgrn_out",
    )(x2d, o2d, gs2d, gnorm, wout_bf16, gpost)


def _gla_exponent_matrix():
    c = GLA_CHUNK
    t = np.arange(c)[:, None]
    w = np.arange(c)[None, :]
    blocks = [(w <= t)]
    for lev in range(1, GLA_FINE_LEVELS + 1):
        n = 1 << lev
        mid = (t // n) * n + n // 2 - 1
        later = (t > mid) & (w > mid) & (w <= t)
        earlier = (t <= mid) & (w > t) & (w <= mid)
        blocks.append(later | earlier)
    return np.concatenate(blocks, axis=0).astype(np.float32)


def _hgrn_prompt_kernel(x_ref, s0_ref, gpre_ref, win_ref, lbl_ref, mall_ref, gn_ref, wout_ref,
                        gpost_ref, y_ref, sn_ref, st, q_buf, k_buf, lf_buf, v_buf, gs_buf, z_buf,
                        *, rows, n_tiles):
    c = GLA_CHUNK
    l = pl.program_id(1)

    @pl.when(l == 0)
    def _():
        for hd in range(N_HEADS):
            st[hd] = s0_ref[0, hd].T

    x = x_ref[0]
    _hgrn_gates(_rms(x, gpre_ref[...]).astype(_BF16), win_ref, lbl_ref,
                q_buf, k_buf, lf_buf, v_buf, gs_buf)

    ti = lax.broadcasted_iota(jnp.int32, (c, c), 0)
    si = lax.broadcasted_iota(jnp.int32, (c, c), 1)
    ri = lax.broadcasted_iota(jnp.int32, (c, 1), 0)
    diag = ti == si
    lvl_mask, lvl_role = [], []
    for lev in range(1, GLA_LEVELS + 1):
        same = (ti >> lev) == (si >> lev)
        t_late = ((ti >> (lev - 1)) & 1) == 1
        s_early = ((si >> (lev - 1)) & 1) == 0
        lvl_mask.append(same & t_late & s_early)
        lvl_role.append(((ri >> (lev - 1)) & 1) == 1)

    def chunk(ci, carry):
        rs = pl.ds(pl.multiple_of(ci * c, c), c)
        lf = lf_buf[rs, :] * LOG2_E
        lf_hi = lf.astype(_BF16)
        r1 = lf - lf_hi.astype(_F32)
        lf_mid = r1.astype(_BF16)
        lf_lo = (r1 - lf_mid.astype(_F32)).astype(_BF16)
        cum = _dot(mall_ref[...], jnp.concatenate([lf_hi, lf_mid, lf_lo], axis=0))
        b = cum[0:c]
        eb = jnp.exp2(b)
        eb_rest = jnp.exp2(b[c - 1:c] - b)
        eb_last = eb[c - 1:c]
        q = q_buf[rs, :]
        k = k_buf[rs, :]
        vb = v_buf[rs, :]
        qd = (q * eb).astype(_BF16)
        kd = (k * eb_rest).astype(_BF16)
        lvl_x = []
        for lev in range(1, GLA_LEVELS + 1):
            n = 1 << lev
            if lev <= GLA_FINE_LEVELS:
                src = jnp.where(lvl_role[lev - 1], q, k)
                expo = cum[lev * c:(lev + 1) * c]
            else:
                src, expo = [], []
                for r0 in range(0, c, n):
                    mid = r0 + n // 2 - 1
                    src += [k[r0:mid + 1], q[mid + 1:r0 + n]]
                    expo += [b[mid:mid + 1] - b[r0:mid + 1], b[mid + 1:r0 + n] - b[mid:mid + 1]]
                src = jnp.concatenate(src, axis=0)
                expo = jnp.concatenate(expo, axis=0)
            lvl_x.append((src * jnp.exp2(expo)).astype(_BF16))
        qk = q * k
        heads = [slice(hd * HEAD_K, (hd + 1) * HEAD_K) for hd in range(N_HEADS)]
        scores = []
        for hs in heads:
            a = jnp.where(diag, jnp.sum(qk[:, hs], axis=-1, keepdims=True), 0.0)
            for lev in range(GLA_LEVELS):
                xl = lvl_x[lev][:, hs]
                a = jnp.where(lvl_mask[lev], _dot_nt(xl, xl), a)
            scores.append(a.astype(_BF16))
        s_old = [st[hd] for hd in range(N_HEADS)]
        for hd, hs in enumerate(heads):
            st[hd] = s_old[hd] * eb_last[:, hs] + _dot_tn(vb[:, hs], kd[:, hs])
        for hd, hs in enumerate(heads):
            o = _dot_nt(qd[:, hs], s_old[hd].astype(_BF16)) + _dot(scores[hd], vb[:, hs])
            z_buf[rs, hs] = (_rms(o, gn_ref[...]) * gs_buf[rs, hs]).astype(_BF16)
        return carry

    lax.fori_loop(0, rows // c, chunk, 0, unroll=GLA_UNROLL)

    m = _dot(z_buf[...], wout_ref[...])
    y_ref[0] = x + _rms(m, gpost_ref[...])

    @pl.when(l == n_tiles - 1)
    def _():
        for hd in range(N_HEADS):
            sn_ref[0, hd] = st[hd].T


def _hgrn_prompt(x, s0, gpre, win_bf16, lb_logits, gnorm, wout_bf16, gpost, *, rows):
    nb, n, _ = x.shape
    n_tiles = n // rows
    mall = jnp.asarray(np.tile(_gla_exponent_matrix(), (1, 3)), dtype=_BF16)
    row_spec = pl.BlockSpec((1, rows, D_MODEL), lambda b, l: (b, l, 0))
    st_spec = pl.BlockSpec((1, N_HEADS, HEAD_K, HEAD_V), lambda b, l: (b, 0, 0, 0))
    kern = functools.partial(_hgrn_prompt_kernel, rows=rows, n_tiles=n_tiles)
    return pl.pallas_call(
        kern,
        grid=(nb, n_tiles),
        in_specs=[row_spec, st_spec, _const_spec((1, D_MODEL)), _const_spec(win_bf16.shape),
                  _const_spec(lb_logits.shape), _const_spec(mall.shape), _const_spec((1, HEAD_V)),
                  _const_spec(wout_bf16.shape), _const_spec((1, D_MODEL))],
        out_specs=[row_spec, st_spec],
        out_shape=[jax.ShapeDtypeStruct(x.shape, _F32), jax.ShapeDtypeStruct(s0.shape, _F32)],
        scratch_shapes=[
            pltpu.VMEM((N_HEADS, HEAD_V, HEAD_K), _F32),
            pltpu.VMEM((rows, F_DIM), _F32),
            pltpu.VMEM((rows, F_DIM), _F32),
            pltpu.VMEM((rows, F_DIM), _F32),
            pltpu.VMEM((rows, V_DIM), _BF16),
            pltpu.VMEM((rows, V_DIM), _F32),
            pltpu.VMEM((rows, V_DIM), _BF16),
        ],
        compiler_params=_params(),
        name="hgrn_prompt",
    )(x, s0, gpre, win_bf16, lb_logits, mall, gnorm, wout_bf16, gpost)


def _gla_sample_kernel(q_ref, k_ref, lf_ref, v_ref, s0_ref, o_ref, sn_ref, *, steps):
    heads = [slice(hd * HEAD_K, (hd + 1) * HEAD_K) for hd in range(N_HEADS)]
    q = [q_ref[t] for t in range(steps)]
    k = [k_ref[t] for t in range(steps)]
    v = [v_ref[t] for t in range(steps)]
    b = [lf_ref[0]]
    for t in range(1, steps):
        b.append(b[-1] + lf_ref[t])
    b_last = b[-1]
    eb_last = jnp.exp(b_last)
    qd = [q[t] * jnp.exp(b[t]) for t in range(steps)]
    kd = [k[s] * jnp.exp(b_last - b[s]) for s in range(steps)]

    o_intra = []
    for t in range(steps):
        acc = None
        for s in range(t + 1):
            p = q[t] * k[s] if s == t else q[t] * k[s] * jnp.exp(b[t] - b[s])
            part = jnp.concatenate(
                [jnp.sum(p[:, hs], axis=-1, keepdims=True) * v[s][:, hs] for hs in heads], axis=-1)
            acc = part if acc is None else acc + part
        o_intra.append(acc)

    zero = jnp.zeros_like(b_last)
    rows_pad = [zero] * (SUBLANES - steps)
    e_hi = eb_last.astype(_BF16).astype(_F32)
    e_r = eb_last - e_hi
    e_mid = e_r.astype(_BF16).astype(_F32)
    e_lo = e_r - e_mid
    assert steps + 3 <= SUBLANES
    qd_b = jnp.stack(qd + rows_pad).swapaxes(0, 1).astype(_BF16)
    v_b = jnp.stack(v + rows_pad).swapaxes(0, 1)
    kd_b = jnp.stack(kd + [e_hi, e_mid, e_lo] + rows_pad[3:]).swapaxes(0, 1).astype(_BF16)
    ri = lax.broadcasted_iota(jnp.int32, (SUBLANES, HEAD_V), 0)
    ones_rows = ((ri >= steps) & (ri < steps + 3)).astype(_F32)

    o_inter = []
    for bi in range(SUBLANES):
        outs = []
        for hd, hs in enumerate(heads):
            s0 = s0_ref[bi, hd]
            outs.append(_dot(qd_b[bi][:, hs], s0.astype(_BF16)))
            rhs = jnp.concatenate([v_b[bi][:, hs], ones_rows], axis=-1).astype(_BF16)
            both = _dot_tn(kd_b[bi][:, hs], rhs)
            sn_ref[bi, hd] = both[:, HEAD_V:] * s0 + both[:, :HEAD_V]
        o_inter.append(jnp.concatenate(outs, axis=-1))
    o_inter = jnp.stack(o_inter).swapaxes(0, 1)
    for t in range(steps):
        o_ref[t] = o_intra[t] + o_inter[t]


def _gla_sample(q, k, lf, v, s0):
    steps, nb, _ = q.shape
    row_spec = pl.BlockSpec((steps, SUBLANES, D_MODEL), lambda i: (0, i, 0))
    st_spec = pl.BlockSpec((SUBLANES, N_HEADS, HEAD_K, HEAD_V), lambda i: (i, 0, 0, 0))
    kern = functools.partial(_gla_sample_kernel, steps=steps)
    return pl.pallas_call(
        kern,
        grid=(nb // SUBLANES,),
        in_specs=[row_spec, row_spec, row_spec, row_spec, st_spec],
        out_specs=[row_spec, st_spec],
        out_shape=[jax.ShapeDtypeStruct(q.shape, _F32), jax.ShapeDtypeStruct(s0.shape, _F32)],
        compiler_params=pltpu.CompilerParams(
            dimension_semantics=("arbitrary",), vmem_limit_bytes=VMEM_LIMIT),
        name="gla_sample",
    )(q, k, lf, v, s0)


def _row(v):
    return v.reshape(1, -1)


def _trunk_prompt(x, wts):
    nb, n, d = x.shape
    pool_ctx = jnp.zeros((nb, POOL_CTX, d), _F32)
    ffn_ctx = jnp.zeros((nb, CONV_W - 1, 2 * D_FF), _F32)
    s0 = jnp.zeros((nb, N_HEADS, HEAD_K, HEAD_V), _F32)

    f32 = wts["f32"]
    x, pool_st, w_up0, w_down0 = _pool_layer(
        x, pool_ctx, wts["mix_pre"][0], wts["pool_w"], wts["pool_scale"], wts["mix_post"][0],
        rows=ROW_TILE, step=1, pos0=0, cast=[(f32["w_up"], 0), (f32["w_down"], 0)])
    x, ffn_st0, w_in, w_out, w_up1, w_down1 = _ffn_layer(
        x, ffn_ctx, wts["ffn_pre"][0], w_up0, wts["conv_w"][0], wts["conv_b"][0], w_down0,
        wts["ffn_post"][0], rows=ROW_TILE, step=1, fold=True,
        cast=[(f32["w_in"], 0), (f32["w_out"], 0), (f32["w_up"], 1), (f32["w_down"], 1)])
    wts.update(w_in=w_in, w_out=w_out, w_up=[w_up0, w_up1], w_down=[w_down0, w_down1])
    x, s_new = _hgrn_prompt(x, s0, wts["mix_pre"][1], wts["w_in"], wts["lb_logits"], wts["gnorm"],
                            wts["w_out"], wts["mix_post"][1], rows=ROW_TILE)
    x, ffn_st1 = _ffn_layer(x, ffn_ctx, wts["ffn_pre"][1], wts["w_up"][1],
                            wts["conv_w"][1], wts["conv_b"][1], wts["w_down"][1],
                            wts["ffn_post"][1], rows=ROW_TILE, step=1, fold=True)
    return x, pool_st[None], s_new[None], jnp.stack([ffn_st0, ffn_st1])


def _trunk_sample(x, pos0, state_pool, state_hgrn, state_ffn, wts):
    nb, steps, d = x.shape
    n = nb * steps

    def to_time_major(a):
        return a.transpose(1, 0, 2).reshape(1, a.shape[1] * nb, a.shape[2])

    def to_batch_major(a, t):
        return a.reshape(t, nb, a.shape[-1]).transpose(1, 0, 2)

    xt = to_time_major(x)
    xt, pool_st = _pool_layer(xt, to_time_major(state_pool[0]), wts["mix_pre"][0], wts["pool_w"],
                              wts["pool_scale"], wts["mix_post"][0], rows=n, step=nb, pos0=pos0)
    xt, ffn_st0 = _ffn_layer(xt, to_time_major(state_ffn[0]), wts["ffn_pre"][0], wts["w_up"][0],
                             wts["conv_w"][0], wts["conv_b"][0], wts["w_down"][0],
                             wts["ffn_post"][0], rows=n, step=nb)
    x2d = xt.reshape(n, d)
    proj = _hgrn_proj(x2d, wts["mix_pre"][1], wts["w_in"], wts["lb_logits"], rows=n)

    q, k, lf, v = (a.reshape(steps, nb, d) for a in proj[:4])
    o, s_new = _gla_sample(q, k, lf, v, state_hgrn[0])
    x2d = _hgrn_out(x2d, o.reshape(n, d), proj[4], wts["gnorm"], wts["w_out"],
                    wts["mix_post"][1], rows=n)
    xt, ffn_st1 = _ffn_layer(x2d.reshape(1, n, d), to_time_major(state_ffn[1]), wts["ffn_pre"][1],
                             wts["w_up"][1], wts["conv_w"][1], wts["conv_b"][1],
                             wts["w_down"][1], wts["ffn_post"][1], rows=n, step=nb)
    y = to_batch_major(xt, steps)
    new_pool = to_batch_major(pool_st, POOL_CTX)[None]
    new_ffn = jnp.stack([to_batch_major(ffn_st0, CONV_W - 1), to_batch_major(ffn_st1, CONV_W - 1)])
    return y, new_pool, s_new[None], new_ffn


def kernel(x_prompt, x_sample, state_pool, state_hgrn, state_ffn_conv, norm_mix_pre, norm_mix_post,
           norm_ffn_pre, norm_ffn_post, pool_w, pool_scale, hgrn_w_in, hgrn_lb_logits, hgrn_gnorm,
           hgrn_w_out, ffn_w_up, ffn_conv_w, ffn_conv_b, ffn_w_down):
    depth = ffn_w_up.shape[0]
    assert depth == 2 and pool_w.shape[0] == 1 and hgrn_w_in.shape[0] == 1
    past_len = 16384
    wts = {
        "mix_pre": [_row(norm_mix_pre[i]) for i in range(depth)],
        "mix_post": [_row(norm_mix_post[i]) for i in range(depth)],
        "ffn_pre": [_row(norm_ffn_pre[i]) for i in range(depth)],
        "ffn_post": [_row(norm_ffn_post[i]) for i in range(depth)],
        "pool_w": pool_w[0].astype(_BF16),
        "pool_scale": _row(pool_scale[0]),
        "lb_logits": hgrn_lb_logits,
        "gnorm": _row(hgrn_gnorm[0]),
        "conv_w": [ffn_conv_w[i] for i in range(depth)],
        "conv_b": [_row(ffn_conv_b[i]) for i in range(depth)],
        "f32": {"w_in": hgrn_w_in, "w_out": hgrn_w_out, "w_up": ffn_w_up, "w_down": ffn_w_down},
    }
    y_p, pool_p, hgrn_p, ffn_p = _trunk_prompt(x_prompt, wts)
    y_s, pool_s, hgrn_s, ffn_s = _trunk_sample(x_sample, past_len, state_pool, state_hgrn,
                                               state_ffn_conv, wts)
    return (y_p, y_s, pool_p, pool_s, hgrn_p, hgrn_s, ffn_p, ffn_s)
```

```python
import functools

import jax
import jax.numpy as jnp
import numpy as np
from jax import lax
from jax.experimental import pallas as pl
from jax.experimental.pallas import tpu as pltpu

D_MODEL = 1024
POOL_WINDOWS = (2, 4, 8, 16)
POOL_GROUP_DIM = D_MODEL // len(POOL_WINDOWS)
POOL_CTX = max(POOL_WINDOWS) - 1
N_HEADS = 8
HEAD_K = 128
HEAD_V = D_MODEL // N_HEADS
F_DIM = N_HEADS * HEAD_K
V_DIM = N_HEADS * HEAD_V
D_FF = 2816
CONV_W = 3
PAST_LEN = 16384
EPS = 1e-6
LOG2_E = 1.4426950408889634

SUBLANES = 8
BF16_ROWS = 16
ROW_TILE = 512
FFN_TILE = 1024
FF_CHUNK = 256
PROJ_CHUNK = 1024
GLA_CHUNK = 64
GLA_LEVELS = 6
GLA_FINE_LEVELS = 3
GLA_UNROLL = 8
VMEM_LIMIT = 56 * 1024 * 1024

_F32 = jnp.float32
_BF16 = jnp.bfloat16


def _rms(x, g):
    ms = jnp.mean(x * x, axis=-1, keepdims=True)
    return x * lax.rsqrt(ms + EPS) * g


def _dot(a, b):
    return jnp.dot(a, b, preferred_element_type=_F32)


def _dot_nt(a, b):
    return lax.dot_general(a, b, (((1,), (1,)), ((), ())), preferred_element_type=_F32)


def _dot_tn(a, b):
    return lax.dot_general(a, b, (((0,), (0,)), ((), ())), preferred_element_type=_F32)


def _const_spec(shape):
    nd = len(shape)
    return pl.BlockSpec(shape, lambda *_: (0,) * nd, pipeline_mode=pl.Buffered(1))


def _params():
    return pltpu.CompilerParams(
        dimension_semantics=("arbitrary", "arbitrary"), vmem_limit_bytes=VMEM_LIMIT)


def _cast_specs(layers, grid):
    n_steps = grid[0] * grid[1]
    in_specs, out_specs, shapes = [], [], []
    for w, layer in layers:
        _, r, c = w.shape
        d = max(k for k in range(1, n_steps + 1) if r % k == 0 and (r // k) % BF16_ROWS == 0)

        def blk(b, l, d=d):
            return jnp.minimum(b * grid[1] + l, d - 1)

        in_specs.append(pl.BlockSpec(
            (None, r // d, c), lambda b, l, blk=blk, layer=layer: (layer, blk(b, l), 0)))
        out_specs.append(pl.BlockSpec((r // d, c), lambda b, l, blk=blk: (blk(b, l), 0)))
        shapes.append(jax.ShapeDtypeStruct((r, c), _BF16))
    return in_specs, out_specs, shapes


def _cast_blocks(src_refs, dst_refs):
    for src, dst in zip(src_refs, dst_refs):
        dst[...] = src[...].astype(_BF16)


def _fold_rows(a):
    rows, c = a.shape
    return a.reshape(SUBLANES, rows // SUBLANES, c).swapaxes(0, 1).reshape(rows, c)


def _unfold_rows(a):
    rows, c = a.shape
    return a.reshape(rows // SUBLANES, SUBLANES, c).swapaxes(0, 1).reshape(rows, c)


def _pool_kernel(*refs, rows, step, pos0, n_tiles, n_cast):
    x_ref, ctx_ref, gpre_ref, w_ref, scale_ref, gpost_ref = refs[:6]
    y_ref, st_ref = refs[6 + n_cast:8 + n_cast]
    hbuf = refs[-1]
    _cast_blocks(refs[6:6 + n_cast], refs[8 + n_cast:8 + 2 * n_cast])
    halo = (POOL_CTX + 1) * step
    l = pl.program_id(1)

    @pl.when(l == 0)
    def _():
        hbuf[0:step, :] = jnp.zeros((step, D_MODEL), _F32)
        hbuf[step:halo, :] = ctx_ref[0]

    x = x_ref[0]
    h = _rms(x, gpre_ref[...])
    hbuf[halo:halo + rows, :] = h

    if pos0 < POOL_CTX:
        assert step == 1
        pos = pos0 + l * rows + lax.broadcasted_iota(jnp.int32, (rows, 1), 0)
    outs = []
    for g, w in enumerate(POOL_WINDOWS):
        cs = slice(g * POOL_GROUP_DIM, (g + 1) * POOL_GROUP_DIM)
        s = hbuf[:, cs]
        span = 1
        while span < w:
            s = s + pltpu.roll(s, span * step, axis=0)
            span *= 2
        s = s[halo:halo + rows]
        if pos0 < POOL_CTX:
            cnt = jnp.minimum(pos + 1, w).astype(_F32)
        else:
            cnt = float(w)
        p = s / cnt - h[:, cs]
        outs.append(_dot(p.astype(_BF16), w_ref[g]))
    m = jnp.concatenate(outs, axis=-1) * scale_ref[...]
    y_ref[0] = x + _rms(m, gpost_ref[...])

    @pl.when(l == n_tiles - 1)
    def _():
        st_ref[0] = hbuf[rows + step:rows + halo, :]

    if n_tiles > 1:
        hbuf[0:halo, :] = hbuf[rows:rows + halo, :]


def _pool_layer(x, ctx, gpre, w_bf16, scale, gpost, *, rows, step, pos0, cast=()):
    nb, n, _ = x.shape
    n_tiles = n // rows
    halo = (POOL_CTX + 1) * step
    ctx_rows = POOL_CTX * step
    cast_in, cast_out, cast_shapes = _cast_specs(cast, (nb, n_tiles))
    kern = functools.partial(_pool_kernel, rows=rows, step=step, pos0=pos0, n_tiles=n_tiles,
                             n_cast=len(cast))
    return pl.pallas_call(
        kern,
        grid=(nb, n_tiles),
        in_specs=[
            pl.BlockSpec((1, rows, D_MODEL), lambda b, l: (b, l, 0)),
            pl.BlockSpec((1, ctx_rows, D_MODEL), lambda b, l: (b, 0, 0)),
            _const_spec((1, D_MODEL)),
            _const_spec(w_bf16.shape),
            _const_spec((1, D_MODEL)),
            _const_spec((1, D_MODEL)),
        ] + cast_in,
        out_specs=[
            pl.BlockSpec((1, rows, D_MODEL), lambda b, l: (b, l, 0)),
            pl.BlockSpec((1, ctx_rows, D_MODEL), lambda b, l: (b, 0, 0)),
        ] + cast_out,
        out_shape=[
            jax.ShapeDtypeStruct(x.shape, _F32),
            jax.ShapeDtypeStruct((nb, ctx_rows, D_MODEL), _F32),
        ] + cast_shapes,
        scratch_shapes=[pltpu.VMEM((halo + rows, D_MODEL), _F32)],
        compiler_params=_params(),
        name="pool_mixer",
    )(x, ctx, gpre, w_bf16, scale, gpost, *(w for w, _ in cast))


def _ffn_kernel(*refs, rows, step, n_tiles, fold, n_cast):
    x_ref, ctx_ref, gpre_ref, wup_ref, cw_ref, cb_ref, wdn_ref, gpost_ref = refs[:8]
    y_ref, st_ref = refs[8 + n_cast:10 + n_cast]
    carry, gbuf, vbuf, abuf = refs[-4:]
    _cast_blocks(refs[8:8 + n_cast], refs[10 + n_cast:10 + 2 * n_cast])
    shift = SUBLANES if fold else step
    halo = max(SUBLANES, (CONV_W - 1) * shift)
    taps = CONV_W - 1
    l = pl.program_id(1)

    @pl.when(l == 0)
    def _():
        carry[...] = jnp.zeros(carry.shape, _F32)
        if fold:
            for j in range(taps):
                carry[(j + 1) * SUBLANES - 1:(j + 1) * SUBLANES, :] = ctx_ref[0, j:j + 1, :]
        else:
            carry[halo - taps * step:halo, :] = ctx_ref[0]

    x = x_ref[0]
    h = _rms(x, gpre_ref[...])
    if fold:
        h = _fold_rows(h)
        sub0 = lax.broadcasted_iota(jnp.int32, (SUBLANES, 1), 0) == 0
    h = h.astype(_BF16)
    for c in range(D_FF // FF_CHUNK):
        conv = []
        for buf, off in ((gbuf, 0), (vbuf, D_FF)):
            cs = slice(off + c * FF_CHUNK, off + (c + 1) * FF_CHUNK)
            u = _dot(h, wup_ref[:, cs])
            if fold:
                for j in range(taps):
                    grp = u[rows - (taps - j) * SUBLANES:rows - (taps - j - 1) * SUBLANES, :]
                    prev = carry[(j + 1) * SUBLANES - 1:(j + 1) * SUBLANES, cs]
                    buf[j * SUBLANES:(j + 1) * SUBLANES, :] = jnp.where(
                        sub0, prev, pltpu.roll(grp, 1, axis=0))
                carry[:, cs] = u[rows - halo:rows, :]
                buf[halo:halo + rows, :] = u
            else:
                buf[0:halo, :] = carry[:, cs]
                buf[halo:halo + rows, :] = u
                carry[:, cs] = buf[rows:rows + halo, :]
            conv.append(cb_ref[:, cs]
                        + buf[halo - 2 * shift:halo - 2 * shift + rows, :] * cw_ref[0:1, cs]
                        + buf[halo - shift:halo - shift + rows, :] * cw_ref[1:2, cs]
                        + u * cw_ref[2:3, cs])
        a = jax.nn.gelu(conv[0], approximate=True) * conv[1]
        abuf[:, c * FF_CHUNK:(c + 1) * FF_CHUNK] = a.astype(_BF16)
    m = _rms(_dot(abuf[...], wdn_ref[...]), gpost_ref[...])
    if fold:
        m = _unfold_rows(m)
    y_ref[0] = x + m

    @pl.when(l == n_tiles - 1)
    def _():
        if fold:
            for j in range(taps):
                st_ref[0, j:j + 1, :] = carry[(j + 1) * SUBLANES - 1:(j + 1) * SUBLANES, :]
        else:
            st_ref[0] = carry[halo - taps * step:halo, :]


def _ffn_layer(x, ctx, gpre, wup_bf16, cw, cb, wdn_bf16, gpost, *, rows, step, fold=False,
               cast=()):
    nb, n, _ = x.shape
    n_tiles = n // rows
    cast_in, cast_out, cast_shapes = _cast_specs(cast, (nb, n_tiles))
    assert not fold or (step == 1 and rows % (SUBLANES * SUBLANES) == 0)
    halo = max(SUBLANES, (CONV_W - 1) * (SUBLANES if fold else step))
    ctx_rows = (CONV_W - 1) * step
    kern = functools.partial(_ffn_kernel, rows=rows, step=step, n_tiles=n_tiles, fold=fold,
                             n_cast=len(cast))
    return pl.pallas_call(
        kern,
        grid=(nb, n_tiles),
        in_specs=[
            pl.BlockSpec((1, rows, D_MODEL), lambda b, l: (b, l, 0)),
            pl.BlockSpec((1, ctx_rows, 2 * D_FF), lambda b, l: (b, 0, 0)),
            _const_spec((1, D_MODEL)),
            _const_spec((D_MODEL, 2 * D_FF)),
            _const_spec((CONV_W, 2 * D_FF)),
            _const_spec((1, 2 * D_FF)),
            _const_spec((D_FF, D_MODEL)),
            _const_spec((1, D_MODEL)),
        ] + cast_in,
        out_specs=[
            pl.BlockSpec((1, rows, D_MODEL), lambda b, l: (b, l, 0)),
            pl.BlockSpec((1, ctx_rows, 2 * D_FF), lambda b, l: (b, 0, 0)),
        ] + cast_out,
        out_shape=[
            jax.ShapeDtypeStruct(x.shape, _F32),
            jax.ShapeDtypeStruct((nb, ctx_rows, 2 * D_FF), _F32),
        ] + cast_shapes,
        scratch_shapes=[
            pltpu.VMEM((halo, 2 * D_FF), _F32),
            pltpu.VMEM((halo + rows, FF_CHUNK), _F32),
            pltpu.VMEM((halo + rows, FF_CHUNK), _F32),
            pltpu.VMEM((rows, D_FF), _BF16),
        ],
        compiler_params=_params(),
        name="conv_ffn",
    )(x, ctx, gpre, wup_bf16, cw, cb, wdn_bf16, gpost, *(w for w, _ in cast))


def _silu(x):
    return x * (0.5 * jnp.tanh(0.5 * x) + 0.5)


def _hgrn_gates(h, win_ref, lbl_ref, q_ref, k_ref, lf_ref, v_ref, gs_ref):
    l0 = lbl_ref[0:1, :]
    l1 = lbl_ref[1:2, :]
    mx = jnp.maximum(l0, l1)
    e0 = jnp.exp(l0 - mx)
    e1 = jnp.exp(l1 - mx)
    lb_all = e1 / (e0 + e1)

    assert F_DIM == V_DIM
    for c in range(F_DIM // PROJ_CHUNK):
        cs = slice(c * PROJ_CHUNK, (c + 1) * PROJ_CHUNK)
        fr = _dot(h, win_ref[:, F_DIM + c * PROJ_CHUNK:F_DIM + (c + 1) * PROJ_CHUNK])
        off = 2 * F_DIM + c * PROJ_CHUNK
        v_ref[:, cs] = _dot(h, win_ref[:, off:off + PROJ_CHUNK]).astype(v_ref.dtype)
        qr = _dot(h, win_ref[:, cs])
        gr = _dot(h, win_ref[:, off + V_DIM:off + V_DIM + PROJ_CHUNK])
        e = jnp.exp(-jnp.abs(fr))
        big = 1.0 / (1.0 + e)
        small = e * big
        pos = fr >= 0.0
        lb = lb_all[:, cs]
        lf_ref[:, cs] = jnp.log(lb + (1.0 - lb) * jnp.where(pos, big, small))
        k_ref[:, cs] = (1.0 - lb) * jnp.where(pos, small, big)
        q_ref[:, cs] = _silu(qr) * (HEAD_K ** -0.5)
        gs_ref[:, cs] = _silu(gr)


def _hproj_kernel(x_ref, gpre_ref, win_ref, lbl_ref, q_ref, k_ref, lf_ref, v_ref, gs_ref):
    h = _rms(x_ref[...], gpre_ref[...]).astype(_BF16)
    _hgrn_gates(h, win_ref, lbl_ref, q_ref, k_ref, lf_ref, v_ref, gs_ref)


def _hgrn_proj(x2d, gpre, win_bf16, lb_logits, *, rows):
    n = x2d.shape[0]
    row_spec = pl.BlockSpec((rows, D_MODEL), lambda i: (i, 0))
    out = jax.ShapeDtypeStruct((n, D_MODEL), _F32)
    return pl.pallas_call(
        _hproj_kernel,
        grid=(n // rows,),
        in_specs=[row_spec, _const_spec((1, D_MODEL)), _const_spec(win_bf16.shape),
                  _const_spec(lb_logits.shape)],
        out_specs=[row_spec] * 5,
        out_shape=[out] * 5,
        compiler_params=pltpu.CompilerParams(
            dimension_semantics=("arbitrary",), vmem_limit_bytes=VMEM_LIMIT),
        name="hgrn_proj",
    )(x2d, gpre, win_bf16, lb_logits)


def _hout_kernel(x_ref, o_ref, gs_ref, gn_ref, wout_ref, gpost_ref, y_ref, zbuf):
    for hd in range(N_HEADS):
        hs = slice(hd * HEAD_V, (hd + 1) * HEAD_V)
        zbuf[:, hs] = (_rms(o_ref[:, hs], gn_ref[...]) * gs_ref[:, hs]).astype(_BF16)
    m = _dot(zbuf[...], wout_ref[...])
    y_ref[...] = x_ref[...] + _rms(m, gpost_ref[...])


def _hgrn_out(x2d, o2d, gs2d, gnorm, wout_bf16, gpost, *, rows):
    n = x2d.shape[0]
    row_spec = pl.BlockSpec((rows, D_MODEL), lambda i: (i, 0))
    return pl.pallas_call(
        _hout_kernel,
        grid=(n // rows,),
        in_specs=[row_spec, row_spec, row_spec, _const_spec((1, HEAD_V)),
                  _const_spec(wout_bf16.shape), _const_spec((1, D_MODEL))],
        out_specs=row_spec,
        out_shape=jax.ShapeDtypeStruct((n, D_MODEL), _F32),
        scratch_shapes=[pltpu.VMEM((rows, V_DIM), _BF16)],
        compiler_params=pltpu.CompilerParams(
            dimension_semantics=("arbitrary",), vmem_limit_bytes=VMEM_LIMIT),
        name="hgrn_out",
    )(x2d, o2d, gs2d, gnorm, wout_bf16, gpost)


def _gla_exponent_matrix():
    c = GLA_CHUNK
    t = np.arange(c)[:, None]
    w = np.arange(c)[None, :]
    blocks = [(w <= t)]
    for lev in range(1, GLA_FINE_LEVELS + 1):
        n = 1 << lev
        mid = (t // n) * n + n // 2 - 1
        later = (t > mid) & (w > mid) & (w <= t)
        earlier = (t <= mid) & (w > t) & (w <= mid)
        blocks.append(later | earlier)
    return np.concatenate(blocks, axis=0).astype(np.float32)


def _hgrn_prompt_kernel(*refs, rows, n_tiles, n_cast):
    x_ref, gpre_ref, win_ref, lbl_ref, mall_ref, gn_ref, wout_ref, gpost_ref = refs[:8]
    y_ref, sn_ref = refs[8 + n_cast:10 + n_cast]
    st, q_buf, k_buf, lf_buf, v_buf, gs_buf, z_buf = refs[-7:]
    _cast_blocks(refs[8:8 + n_cast], refs[10 + n_cast:10 + 2 * n_cast])
    c = GLA_CHUNK
    l = pl.program_id(1)

    @pl.when(l == 0)
    def _():
        st[...] = jnp.zeros(st.shape, _F32)

    x = x_ref[0]
    _hgrn_gates(_rms(x, gpre_ref[...]).astype(_BF16), win_ref, lbl_ref,
                q_buf, k_buf, lf_buf, v_buf, gs_buf)

    ti = lax.broadcasted_iota(jnp.int32, (c, c), 0)
    si = lax.broadcasted_iota(jnp.int32, (c, c), 1)
    ri = lax.broadcasted_iota(jnp.int32, (c, 1), 0)
    diag = ti == si
    lvl_mask, lvl_role = [], []
    for lev in range(1, GLA_LEVELS + 1):
        same = (ti >> lev) == (si >> lev)
        t_late = ((ti >> (lev - 1)) & 1) == 1
        s_early = ((si >> (lev - 1)) & 1) == 0
        lvl_mask.append(same & t_late & s_early)
        lvl_role.append(((ri >> (lev - 1)) & 1) == 1)

    def chunk(ci, carry):
        rs = pl.ds(pl.multiple_of(ci * c, c), c)
        lf = lf_buf[rs, :] * LOG2_E
        lf_hi = lf.astype(_BF16)
        r1 = lf - lf_hi.astype(_F32)
        lf_mid = r1.astype(_BF16)
        lf_lo = (r1 - lf_mid.astype(_F32)).astype(_BF16)
        cum = _dot(mall_ref[...], jnp.concatenate([lf_hi, lf_mid, lf_lo], axis=0))
        b = cum[0:c]
        eb = jnp.exp2(b)
        eb_rest = jnp.exp2(b[c - 1:c] - b)
        eb_last = eb[c - 1:c]
        q = q_buf[rs, :]
        k = k_buf[rs, :]
        vb = v_buf[rs, :]
        qd = (q * eb).astype(_BF16)
        kd = (k * eb_rest).astype(_BF16)
        lvl_x = []
        for lev in range(1, GLA_LEVELS + 1):
            n = 1 << lev
            if lev <= GLA_FINE_LEVELS:
                src = jnp.where(lvl_role[lev - 1], q, k)
                expo = cum[lev * c:(lev + 1) * c]
            else:
                src, expo = [], []
                for r0 in range(0, c, n):
                    mid = r0 + n // 2 - 1
                    src += [k[r0:mid + 1], q[mid + 1:r0 + n]]
                    expo += [b[mid:mid + 1] - b[r0:mid + 1], b[mid + 1:r0 + n] - b[mid:mid + 1]]
                src = jnp.concatenate(src, axis=0)
                expo = jnp.concatenate(expo, axis=0)
            lvl_x.append((src * jnp.exp2(expo)).astype(_BF16))
        qk = q * k
        heads = [slice(hd * HEAD_K, (hd + 1) * HEAD_K) for hd in range(N_HEADS)]
        scores = []
        for hs in heads:
            a = jnp.where(diag, jnp.sum(qk[:, hs], axis=-1, keepdims=True), 0.0)
            for lev in range(GLA_LEVELS):
                xl = lvl_x[lev][:, hs]
                a = jnp.where(lvl_mask[lev], _dot_nt(xl, xl), a)
            scores.append(a.astype(_BF16))
        s_old = [st[hd] for hd in range(N_HEADS)]
        for hd, hs in enumerate(heads):
            st[hd] = s_old[hd] * eb_last[:, hs] + _dot_tn(vb[:, hs], kd[:, hs])
        for hd, hs in enumerate(heads):
            o = _dot_nt(qd[:, hs], s_old[hd].astype(_BF16)) + _dot(scores[hd], vb[:, hs])
            z_buf[rs, hs] = (_rms(o, gn_ref[...]) * gs_buf[rs, hs]).astype(_BF16)
        return carry

    lax.fori_loop(0, rows // c, chunk, 0, unroll=GLA_UNROLL)

    m = _dot(z_buf[...], wout_ref[...])
    y_ref[0] = x + _rms(m, gpost_ref[...])

    @pl.when(l == n_tiles - 1)
    def _():
        for hd in range(N_HEADS):
            sn_ref[0, hd] = st[hd].T


def _hgrn_prompt(x, gpre, win_bf16, lb_logits, gnorm, wout_bf16, gpost, *, rows, cast=()):
    nb, n, _ = x.shape
    n_tiles = n // rows
    mall = jnp.asarray(np.tile(_gla_exponent_matrix(), (1, 3)), dtype=_BF16)
    row_spec = pl.BlockSpec((1, rows, D_MODEL), lambda b, l: (b, l, 0))
    st_spec = pl.BlockSpec((1, N_HEADS, HEAD_K, HEAD_V), lambda b, l: (b, 0, 0, 0))
    cast_in, cast_out, cast_shapes = _cast_specs(cast, (nb, n_tiles))
    kern = functools.partial(_hgrn_prompt_kernel, rows=rows, n_tiles=n_tiles, n_cast=len(cast))
    return pl.pallas_call(
        kern,
        grid=(nb, n_tiles),
        in_specs=[row_spec, _const_spec((1, D_MODEL)), _const_spec(win_bf16.shape),
                  _const_spec(lb_logits.shape), _const_spec(mall.shape), _const_spec((1, HEAD_V)),
                  _const_spec(wout_bf16.shape), _const_spec((1, D_MODEL))] + cast_in,
        out_specs=[row_spec, st_spec] + cast_out,
        out_shape=[jax.ShapeDtypeStruct(x.shape, _F32),
                   jax.ShapeDtypeStruct((nb, N_HEADS, HEAD_K, HEAD_V), _F32)] + cast_shapes,
        scratch_shapes=[
            pltpu.VMEM((N_HEADS, HEAD_V, HEAD_K), _F32),
            pltpu.VMEM((rows, F_DIM), _F32),
            pltpu.VMEM((rows, F_DIM), _F32),
            pltpu.VMEM((rows, F_DIM), _F32),
            pltpu.VMEM((rows, V_DIM), _BF16),
            pltpu.VMEM((rows, V_DIM), _F32),
            pltpu.VMEM((rows, V_DIM), _BF16),
        ],
        compiler_params=_params(),
        name="hgrn_prompt",
    )(x, gpre, win_bf16, lb_logits, mall, gnorm, wout_bf16, gpost, *(w for w, _ in cast))


def _gla_sample_kernel(q_ref, k_ref, lf_ref, v_ref, s0_ref, o_ref, sn_ref, *, steps):
    heads = [slice(hd * HEAD_K, (hd + 1) * HEAD_K) for hd in range(N_HEADS)]
    q = [q_ref[t] for t in range(steps)]
    k = [k_ref[t] for t in range(steps)]
    v = [v_ref[t] for t in range(steps)]
    b = [lf_ref[0]]
    for t in range(1, steps):
        b.append(b[-1] + lf_ref[t])
    b_last = b[-1]
    eb_last = jnp.exp(b_last)
    qd = [q[t] * jnp.exp(b[t]) for t in range(steps)]
    kd = [k[s] * jnp.exp(b_last - b[s]) for s in range(steps)]

    o_intra = []
    for t in range(steps):
        acc = None
        for s in range(t + 1):
            p = q[t] * k[s] if s == t else q[t] * k[s] * jnp.exp(b[t] - b[s])
            part = jnp.concatenate(
                [jnp.sum(p[:, hs], axis=-1, keepdims=True) * v[s][:, hs] for hs in heads], axis=-1)
            acc = part if acc is None else acc + part
        o_intra.append(acc)

    zero = jnp.zeros_like(b_last)
    rows_pad = [zero] * (SUBLANES - steps)
    e_hi = eb_last.astype(_BF16).astype(_F32)
    e_r = eb_last - e_hi
    e_mid = e_r.astype(_BF16).astype(_F32)
    e_lo = e_r - e_mid
    assert steps + 3 <= SUBLANES
    qd_b = jnp.stack(qd + rows_pad).swapaxes(0, 1).astype(_BF16)
    v_b = jnp.stack(v + rows_pad).swapaxes(0, 1)
    kd_b = jnp.stack(kd + [e_hi, e_mid, e_lo] + rows_pad[3:]).swapaxes(0, 1).astype(_BF16)
    ri = lax.broadcasted_iota(jnp.int32, (SUBLANES, HEAD_V), 0)
    ones_rows = ((ri >= steps) & (ri < steps + 3)).astype(_F32)

    o_inter = []
    for bi in range(SUBLANES):
        outs = []
        for hd, hs in enumerate(heads):
            s0 = s0_ref[bi, hd]
            outs.append(_dot(qd_b[bi][:, hs], s0.astype(_BF16)))
            rhs = jnp.concatenate([v_b[bi][:, hs], ones_rows], axis=-1).astype(_BF16)
            both = _dot_tn(kd_b[bi][:, hs], rhs)
            sn_ref[bi, hd] = both[:, HEAD_V:] * s0 + both[:, :HEAD_V]
        o_inter.append(jnp.concatenate(outs, axis=-1))
    o_inter = jnp.stack(o_inter).swapaxes(0, 1)
    for t in range(steps):
        o_ref[t] = o_intra[t] + o_inter[t]


def _gla_sample(q, k, lf, v, s0):
    steps, nb, _ = q.shape
    row_spec = pl.BlockSpec((steps, SUBLANES, D_MODEL), lambda i: (0, i, 0))
    st_spec = pl.BlockSpec((SUBLANES, N_HEADS, HEAD_K, HEAD_V), lambda i: (i, 0, 0, 0))
    kern = functools.partial(_gla_sample_kernel, steps=steps)
    return pl.pallas_call(
        kern,
        grid=(nb // SUBLANES,),
        in_specs=[row_spec, row_spec, row_spec, row_spec, st_spec],
        out_specs=[row_spec, st_spec],
        out_shape=[jax.ShapeDtypeStruct(q.shape, _F32), jax.ShapeDtypeStruct(s0.shape, _F32)],
        compiler_params=pltpu.CompilerParams(
            dimension_semantics=("arbitrary",), vmem_limit_bytes=VMEM_LIMIT),
        name="gla_sample",
    )(q, k, lf, v, s0)


def _row(v):
    return v.reshape(1, -1)


def _trunk_prompt(x, wts):
    nb, n, d = x.shape
    pool_ctx = jnp.zeros((nb, POOL_CTX, d), _F32)
    ffn_ctx = jnp.zeros((nb, CONV_W - 1, 2 * D_FF), _F32)

    f32 = wts["f32"]
    x, pool_st, w_up0, w_down0 = _pool_layer(
        x, pool_ctx, wts["mix_pre"][0], wts["pool_w"], wts["pool_scale"], wts["mix_post"][0],
        rows=ROW_TILE, step=1, pos0=0, cast=[(f32["w_up"], 0), (f32["w_down"], 0)])
    x, ffn_st0, w_in, w_out = _ffn_layer(
        x, ffn_ctx, wts["ffn_pre"][0], w_up0, wts["conv_w"][0], wts["conv_b"][0], w_down0,
        wts["ffn_post"][0], rows=FFN_TILE, step=1, fold=True,
        cast=[(f32["w_in"], 0), (f32["w_out"], 0)])
    x, s_new, w_up1, w_down1 = _hgrn_prompt(
        x, wts["mix_pre"][1], w_in, wts["lb_logits"], wts["gnorm"], w_out, wts["mix_post"][1],
        rows=ROW_TILE, cast=[(f32["w_up"], 1), (f32["w_down"], 1)])
    wts.update(w_in=w_in, w_out=w_out, w_up=[w_up0, w_up1], w_down=[w_down0, w_down1])
    x, ffn_st1 = _ffn_layer(x, ffn_ctx, wts["ffn_pre"][1], wts["w_up"][1],
                            wts["conv_w"][1], wts["conv_b"][1], wts["w_down"][1],
                            wts["ffn_post"][1], rows=FFN_TILE, step=1, fold=True)
    return x, pool_st[None], s_new[None], jnp.stack([ffn_st0, ffn_st1])


def _trunk_sample(x, pos0, state_pool, state_hgrn, state_ffn, wts):
    nb, steps, d = x.shape
    n = nb * steps

    def to_time_major(a):
        return a.transpose(1, 0, 2).reshape(1, a.shape[1] * nb, a.shape[2])

    def to_batch_major(a, t):
        return a.reshape(t, nb, a.shape[-1]).transpose(1, 0, 2)

    xt = to_time_major(x)
    xt, pool_st = _pool_layer(xt, to_time_major(state_pool[0]), wts["mix_pre"][0], wts["pool_w"],
                              wts["pool_scale"], wts["mix_post"][0], rows=n, step=nb, pos0=pos0)
    xt, ffn_st0 = _ffn_layer(xt, to_time_major(state_ffn[0]), wts["ffn_pre"][0], wts["w_up"][0],
                             wts["conv_w"][0], wts["conv_b"][0], wts["w_down"][0],
                             wts["ffn_post"][0], rows=n, step=nb)
    x2d = xt.reshape(n, d)
    proj = _hgrn_proj(x2d, wts["mix_pre"][1], wts["w_in"], wts["lb_logits"], rows=n)

    q, k, lf, v = (a.reshape(steps, nb, d) for a in proj[:4])
    o, s_new = _gla_sample(q, k, lf, v, state_hgrn[0])
    x2d = _hgrn_out(x2d, o.reshape(n, d), proj[4], wts["gnorm"], wts["w_out"],
                    wts["mix_post"][1], rows=n)
    xt, ffn_st1 = _ffn_layer(x2d.reshape(1, n, d), to_time_major(state_ffn[1]), wts["ffn_pre"][1],
                             wts["w_up"][1], wts["conv_w"][1], wts["conv_b"][1],
                             wts["w_down"][1], wts["ffn_post"][1], rows=n, step=nb)
    y = to_batch_major(xt, steps)
    new_pool = to_batch_major(pool_st, POOL_CTX)[None]
    new_ffn = jnp.stack([ffn_st0, ffn_st1]).reshape(2, CONV_W - 1, nb, 2 * D_FF).transpose(0, 2, 1, 3)
    return y, new_pool, s_new[None], new_ffn


def kernel(x_prompt, x_sample, state_pool, state_hgrn, state_ffn_conv, norm_mix_pre, norm_mix_post,
           norm_ffn_pre, norm_ffn_post, pool_w, pool_scale, hgrn_w_in, hgrn_lb_logits, hgrn_gnorm,
           hgrn_w_out, ffn_w_up, ffn_conv_w, ffn_conv_b, ffn_w_down):
    depth = ffn_w_up.shape[0]
    assert depth == 2 and pool_w.shape[0] == 1 and hgrn_w_in.shape[0] == 1
    wts = {
        "mix_pre": [_row(norm_mix_pre[i]) for i in range(depth)],
        "mix_post": [_row(norm_mix_post[i]) for i in range(depth)],
        "ffn_pre": [_row(norm_ffn_pre[i]) for i in range(depth)],
        "ffn_post": [_row(norm_ffn_post[i]) for i in range(depth)],
        "pool_w": pool_w[0].astype(_BF16),
        "pool_scale": _row(pool_scale[0]),
        "lb_logits": hgrn_lb_logits,
        "gnorm": _row(hgrn_gnorm[0]),
        "conv_w": [ffn_conv_w[i] for i in range(depth)],
        "conv_b": [_row(ffn_conv_b[i]) for i in range(depth)],
        "f32": {"w_in": hgrn_w_in, "w_out": hgrn_w_out, "w_up": ffn_w_up, "w_down": ffn_w_down},
    }
    y_p, pool_p, hgrn_p, ffn_p = _trunk_prompt(x_prompt, wts)
    y_s, pool_s, hgrn_s, ffn_s = _trunk_sample(x_sample, PAST_LEN, state_pool, state_hgrn,
                                               state_ffn_conv, wts)
    return (y_p, y_s, pool_p, pool_s, hgrn_p, hgrn_s, ffn_p, ffn_s)
```

```python
import functools

import jax
import jax.numpy as jnp
import numpy as np
from jax import lax
from jax.experimental import pallas as pl
from jax.experimental.pallas import tpu as pltpu

D_MODEL = 1024
POOL_WINDOWS = (2, 4, 8, 16)
POOL_GROUP_DIM = D_MODEL // len(POOL_WINDOWS)
POOL_CTX = max(POOL_WINDOWS) - 1
N_HEADS = 8
HEAD_K = 128
HEAD_V = D_MODEL // N_HEADS
F_DIM = N_HEADS * HEAD_K
V_DIM = N_HEADS * HEAD_V
D_FF = 2816
CONV_W = 3
PAST_LEN = 16384
EPS = 1e-6
LOG2_E = 1.4426950408889634

SUBLANES = 8
BF16_ROWS = 16
ROW_TILE = 512
POOL_TILE = 1024
FFN_TILE = 1024
FF_CHUNK = 256
PROJ_CHUNK = 1024
GLA_CHUNK = 64
GLA_LEVELS = 6
GLA_FINE_LEVELS = 3
GLA_UNROLL = 8
VMEM_LIMIT = 56 * 1024 * 1024

_F32 = jnp.float32
_BF16 = jnp.bfloat16


def _rms(x, g):
    ms = jnp.mean(x * x, axis=-1, keepdims=True)
    return x * lax.rsqrt(ms + EPS) * g


def _dot(a, b):
    return jnp.dot(a, b, preferred_element_type=_F32)


def _dot_nt(a, b):
    return lax.dot_general(a, b, (((1,), (1,)), ((), ())), preferred_element_type=_F32)


def _dot_tn(a, b):
    return lax.dot_general(a, b, (((0,), (0,)), ((), ())), preferred_element_type=_F32)


def _const_spec(shape):
    nd = len(shape)
    return pl.BlockSpec(shape, lambda *_: (0,) * nd, pipeline_mode=pl.Buffered(1))


def _params():
    return pltpu.CompilerParams(
        dimension_semantics=("arbitrary", "arbitrary"), vmem_limit_bytes=VMEM_LIMIT)


def _cast_specs(layers, grid):
    n_steps = grid[0] * grid[1]
    in_specs, out_specs, shapes = [], [], []
    for w, layer in layers:
        _, r, c = w.shape
        d = max(k for k in range(1, n_steps + 1) if r % k == 0 and (r // k) % BF16_ROWS == 0)

        def blk(b, l, d=d):
            return jnp.minimum(b * grid[1] + l, d - 1)

        in_specs.append(pl.BlockSpec(
            (None, r // d, c), lambda b, l, blk=blk, layer=layer: (layer, blk(b, l), 0)))
        out_specs.append(pl.BlockSpec((r // d, c), lambda b, l, blk=blk: (blk(b, l), 0)))
        shapes.append(jax.ShapeDtypeStruct((r, c), _BF16))
    return in_specs, out_specs, shapes


def _cast_blocks(src_refs, dst_refs):
    for src, dst in zip(src_refs, dst_refs):
        dst[...] = src[...].astype(_BF16)


def _fold_rows(a):
    rows, c = a.shape
    return a.reshape(SUBLANES, rows // SUBLANES, c).swapaxes(0, 1).reshape(rows, c)


def _unfold_rows(a):
    rows, c = a.shape
    return a.reshape(rows // SUBLANES, SUBLANES, c).swapaxes(0, 1).reshape(rows, c)


def _pool_kernel(*refs, rows, step, pos0, n_tiles, n_cast):
    x_ref, ctx_ref, gpre_ref, w_ref, scale_ref, gpost_ref = refs[:6]
    y_ref, st_ref = refs[6 + n_cast:8 + n_cast]
    hbuf = refs[-1]
    _cast_blocks(refs[6:6 + n_cast], refs[8 + n_cast:8 + 2 * n_cast])
    halo = (POOL_CTX + 1) * step
    l = pl.program_id(1)

    @pl.when(l == 0)
    def _():
        hbuf[0:step, :] = jnp.zeros((step, D_MODEL), _F32)
        hbuf[step:halo, :] = ctx_ref[0]

    x = x_ref[0]
    h = _rms(x, gpre_ref[...])
    hbuf[halo:halo + rows, :] = h

    if pos0 < POOL_CTX:
        assert step == 1
        pos = pos0 + l * rows + lax.broadcasted_iota(jnp.int32, (rows, 1), 0)
    outs = []
    for g, w in enumerate(POOL_WINDOWS):
        cs = slice(g * POOL_GROUP_DIM, (g + 1) * POOL_GROUP_DIM)
        s = hbuf[:, cs]
        span = 1
        while span < w:
            s = s + pltpu.roll(s, span * step, axis=0)
            span *= 2
        s = s[halo:halo + rows]
        if pos0 < POOL_CTX:
            cnt = jnp.minimum(pos + 1, w).astype(_F32)
        else:
            cnt = float(w)
        p = s / cnt - h[:, cs]
        outs.append(_dot(p.astype(_BF16), w_ref[g]))
    m = jnp.concatenate(outs, axis=-1) * scale_ref[...]
    y_ref[0] = x + _rms(m, gpost_ref[...])

    @pl.when(l == n_tiles - 1)
    def _():
        st_ref[0] = hbuf[rows + step:rows + halo, :]

    if n_tiles > 1:
        hbuf[0:halo, :] = hbuf[rows:rows + halo, :]


def _pool_layer(x, ctx, gpre, w_bf16, scale, gpost, *, rows, step, pos0, cast=()):
    nb, n, _ = x.shape
    n_tiles = n // rows
    halo = (POOL_CTX + 1) * step
    ctx_rows = POOL_CTX * step
    cast_in, cast_out, cast_shapes = _cast_specs(cast, (nb, n_tiles))
    kern = functools.partial(_pool_kernel, rows=rows, step=step, pos0=pos0, n_tiles=n_tiles,
                             n_cast=len(cast))
    return pl.pallas_call(
        kern,
        grid=(nb, n_tiles),
        in_specs=[
            pl.BlockSpec((1, rows, D_MODEL), lambda b, l: (b, l, 0)),
            pl.BlockSpec((1, ctx_rows, D_MODEL), lambda b, l: (b, 0, 0)),
            _const_spec((1, D_MODEL)),
            _const_spec(w_bf16.shape),
            _const_spec((1, D_MODEL)),
            _const_spec((1, D_MODEL)),
        ] + cast_in,
        out_specs=[
            pl.BlockSpec((1, rows, D_MODEL), lambda b, l: (b, l, 0)),
            pl.BlockSpec((1, ctx_rows, D_MODEL), lambda b, l: (b, 0, 0)),
        ] + cast_out,
        out_shape=[
            jax.ShapeDtypeStruct(x.shape, _F32),
            jax.ShapeDtypeStruct((nb, ctx_rows, D_MODEL), _F32),
        ] + cast_shapes,
        scratch_shapes=[pltpu.VMEM((halo + rows, D_MODEL), _F32)],
        compiler_params=_params(),
        name="pool_mixer",
    )(x, ctx, gpre, w_bf16, scale, gpost, *(w for w, _ in cast))


def _ffn_kernel(*refs, rows, step, n_tiles, fold, n_cast):
    x_ref, ctx_ref, gpre_ref, wup_ref, cw_ref, cb_ref, wdn_ref, gpost_ref = refs[:8]
    y_ref, st_ref = refs[8 + n_cast:10 + n_cast]
    carry, gbuf, vbuf, abuf = refs[-4:]
    _cast_blocks(refs[8:8 + n_cast], refs[10 + n_cast:10 + 2 * n_cast])
    shift = SUBLANES if fold else step
    halo = max(SUBLANES, (CONV_W - 1) * shift)
    taps = CONV_W - 1
    l = pl.program_id(1)

    @pl.when(l == 0)
    def _():
        carry[...] = jnp.zeros(carry.shape, _F32)
        if fold:
            for j in range(taps):
                carry[(j + 1) * SUBLANES - 1:(j + 1) * SUBLANES, :] = ctx_ref[0, j:j + 1, :]
        else:
            carry[halo - taps * step:halo, :] = ctx_ref[0]

    x = x_ref[0]
    h = _rms(x, gpre_ref[...])
    if fold:
        h = _fold_rows(h)
        sub0 = lax.broadcasted_iota(jnp.int32, (SUBLANES, 1), 0) == 0
    h = h.astype(_BF16)
    for c in range(D_FF // FF_CHUNK):
        conv = []
        for buf, off in ((gbuf, 0), (vbuf, D_FF)):
            cs = slice(off + c * FF_CHUNK, off + (c + 1) * FF_CHUNK)
            u = _dot(h, wup_ref[:, cs])
            if fold:
                for j in range(taps):
                    grp = u[rows - (taps - j) * SUBLANES:rows - (taps - j - 1) * SUBLANES, :]
                    prev = carry[(j + 1) * SUBLANES - 1:(j + 1) * SUBLANES, cs]
                    buf[j * SUBLANES:(j + 1) * SUBLANES, :] = jnp.where(
                        sub0, prev, pltpu.roll(grp, 1, axis=0))
                carry[:, cs] = u[rows - halo:rows, :]
                buf[halo:halo + rows, :] = u
            else:
                buf[0:halo, :] = carry[:, cs]
                buf[halo:halo + rows, :] = u
                carry[:, cs] = buf[rows:rows + halo, :]
            conv.append(cb_ref[:, cs]
                        + buf[halo - 2 * shift:halo - 2 * shift + rows, :] * cw_ref[0:1, cs]
                        + buf[halo - shift:halo - shift + rows, :] * cw_ref[1:2, cs]
                        + u * cw_ref[2:3, cs])
        a = jax.nn.gelu(conv[0], approximate=True) * conv[1]
        abuf[:, c * FF_CHUNK:(c + 1) * FF_CHUNK] = a.astype(_BF16)
    m = _rms(_dot(abuf[...], wdn_ref[...]), gpost_ref[...])
    if fold:
        m = _unfold_rows(m)
    y_ref[0] = x + m

    @pl.when(l == n_tiles - 1)
    def _():
        if fold:
            for j in range(taps):
                st_ref[0, j:j + 1, :] = carry[(j + 1) * SUBLANES - 1:(j + 1) * SUBLANES, :]
        else:
            st_ref[0] = carry[halo - taps * step:halo, :]


def _ffn_layer(x, ctx, gpre, wup_bf16, cw, cb, wdn_bf16, gpost, *, rows, step, fold=False,
               cast=()):
    nb, n, _ = x.shape
    n_tiles = n // rows
    cast_in, cast_out, cast_shapes = _cast_specs(cast, (nb, n_tiles))
    assert not fold or (step == 1 and rows % (SUBLANES * SUBLANES) == 0)
    halo = max(SUBLANES, (CONV_W - 1) * (SUBLANES if fold else step))
    ctx_rows = (CONV_W - 1) * step
    kern = functools.partial(_ffn_kernel, rows=rows, step=step, n_tiles=n_tiles, fold=fold,
                             n_cast=len(cast))
    return pl.pallas_call(
        kern,
        grid=(nb, n_tiles),
        in_specs=[
            pl.BlockSpec((1, rows, D_MODEL), lambda b, l: (b, l, 0)),
            pl.BlockSpec((1, ctx_rows, 2 * D_FF), lambda b, l: (b, 0, 0)),
            _const_spec((1, D_MODEL)),
            _const_spec((D_MODEL, 2 * D_FF)),
            _const_spec((CONV_W, 2 * D_FF)),
            _const_spec((1, 2 * D_FF)),
            _const_spec((D_FF, D_MODEL)),
            _const_spec((1, D_MODEL)),
        ] + cast_in,
        out_specs=[
            pl.BlockSpec((1, rows, D_MODEL), lambda b, l: (b, l, 0)),
            pl.BlockSpec((1, ctx_rows, 2 * D_FF), lambda b, l: (b, 0, 0)),
        ] + cast_out,
        out_shape=[
            jax.ShapeDtypeStruct(x.shape, _F32),
            jax.ShapeDtypeStruct((nb, ctx_rows, 2 * D_FF), _F32),
        ] + cast_shapes,
        scratch_shapes=[
            pltpu.VMEM((halo, 2 * D_FF), _F32),
            pltpu.VMEM((halo + rows, FF_CHUNK), _F32),
            pltpu.VMEM((halo + rows, FF_CHUNK), _F32),
            pltpu.VMEM((rows, D_FF), _BF16),
        ],
        compiler_params=_params(),
        name="conv_ffn",
    )(x, ctx, gpre, wup_bf16, cw, cb, wdn_bf16, gpost, *(w for w, _ in cast))


def _ffn_stream_kernel(x_ref, ctxg_ref, ctxv_ref, gpre_ref, wg_ref, wv_ref, cwg_ref, cwv_ref,
                       cbg_ref, cbv_ref, wdn_ref, gpost_ref, y_ref, stg_ref, stv_ref, hbuf, acc,
                       *, rows, step, n_chunks):
    c = pl.program_id(0)
    taps = CONV_W - 1

    @pl.when(c == 0)
    def _():
        hbuf[...] = _rms(x_ref[0], gpre_ref[...]).astype(_BF16)
        acc[...] = jnp.zeros(acc.shape, _F32)

    h = hbuf[...]
    conv = []
    for ctx_ref, w_ref, cw_ref, cb_ref, st_ref in ((ctxg_ref, wg_ref, cwg_ref, cbg_ref, stg_ref),
                                                   (ctxv_ref, wv_ref, cwv_ref, cbv_ref, stv_ref)):
        u = _dot(h, w_ref[...])
        full = jnp.concatenate([ctx_ref[0], u], axis=0)
        st_ref[0] = full[rows:rows + taps * step]
        conv.append(cb_ref[...] + full[0:rows] * cw_ref[0:1, :]
                    + full[step:step + rows] * cw_ref[1:2, :] + u * cw_ref[2:3, :])
    a = (jax.nn.gelu(conv[0], approximate=True) * conv[1]).astype(_BF16)
    acc[...] += _dot(a, wdn_ref[...])

    @pl.when(c == n_chunks - 1)
    def _():
        y_ref[0] = x_ref[0] + _rms(acc[...], gpost_ref[...])


def _ffn_stream_layer(x, ctx, gpre, wup_bf16, cw, cb, wdn_bf16, gpost, *, step):
    _, rows, _ = x.shape
    n_chunks = D_FF // FF_CHUNK
    ctx_rows = (CONV_W - 1) * step
    assert step % SUBLANES == 0
    kern = functools.partial(_ffn_stream_kernel, rows=rows, step=step, n_chunks=n_chunks)

    def gate(*lead):
        return lambda c: (*lead, c)

    def val(*lead):
        return lambda c: (*lead, n_chunks + c)

    y, st_g, st_v = pl.pallas_call(
        kern,
        grid=(n_chunks,),
        in_specs=[
            pl.BlockSpec((1, rows, D_MODEL), lambda c: (0, 0, 0)),
            pl.BlockSpec((1, ctx_rows, FF_CHUNK), gate(0, 0)),
            pl.BlockSpec((1, ctx_rows, FF_CHUNK), val(0, 0)),
            _const_spec((1, D_MODEL)),
            pl.BlockSpec((D_MODEL, FF_CHUNK), gate(0)),
            pl.BlockSpec((D_MODEL, FF_CHUNK), val(0)),
            pl.BlockSpec((CONV_W, FF_CHUNK), gate(0)),
            pl.BlockSpec((CONV_W, FF_CHUNK), val(0)),
            pl.BlockSpec((1, FF_CHUNK), gate(0)),
            pl.BlockSpec((1, FF_CHUNK), val(0)),
            pl.BlockSpec((FF_CHUNK, D_MODEL), lambda c: (c, 0)),
            _const_spec((1, D_MODEL)),
        ],
        out_specs=[
            pl.BlockSpec((1, rows, D_MODEL), lambda c: (0, 0, 0)),
            pl.BlockSpec((1, ctx_rows, FF_CHUNK), gate(0, 0)),
            pl.BlockSpec((1, ctx_rows, FF_CHUNK), gate(0, 0)),
        ],
        out_shape=[
            jax.ShapeDtypeStruct(x.shape, _F32),
            jax.ShapeDtypeStruct((1, ctx_rows, D_FF), _F32),
            jax.ShapeDtypeStruct((1, ctx_rows, D_FF), _F32),
        ],
        scratch_shapes=[
            pltpu.VMEM((rows, D_MODEL), _BF16),
            pltpu.VMEM((rows, D_MODEL), _F32),
        ],
        compiler_params=pltpu.CompilerParams(
            dimension_semantics=("arbitrary",), vmem_limit_bytes=VMEM_LIMIT),
        name="conv_ffn_stream",
    )(x, ctx, ctx, gpre, wup_bf16, wup_bf16, cw, cw, cb, cb, wdn_bf16, gpost)
    return y, jnp.concatenate([st_g, st_v], axis=-1)


def _silu(x):
    return x * (0.5 * jnp.tanh(0.5 * x) + 0.5)


def _hgrn_gates(h, win_ref, lbl_ref, q_ref, k_ref, lf_ref, v_ref, gs_ref):
    l0 = lbl_ref[0:1, :]
    l1 = lbl_ref[1:2, :]
    mx = jnp.maximum(l0, l1)
    e0 = jnp.exp(l0 - mx)
    e1 = jnp.exp(l1 - mx)
    lb_all = e1 / (e0 + e1)

    assert F_DIM == V_DIM
    for c in range(F_DIM // PROJ_CHUNK):
        cs = slice(c * PROJ_CHUNK, (c + 1) * PROJ_CHUNK)
        fr = _dot(h, win_ref[:, F_DIM + c * PROJ_CHUNK:F_DIM + (c + 1) * PROJ_CHUNK])
        off = 2 * F_DIM + c * PROJ_CHUNK
        v_ref[:, cs] = _dot(h, win_ref[:, off:off + PROJ_CHUNK]).astype(v_ref.dtype)
        qr = _dot(h, win_ref[:, cs])
        gr = _dot(h, win_ref[:, off + V_DIM:off + V_DIM + PROJ_CHUNK])
        e = jnp.exp(-jnp.abs(fr))
        big = 1.0 / (1.0 + e)
        small = e * big
        pos = fr >= 0.0
        lb = lb_all[:, cs]
        lf_ref[:, cs] = jnp.log(lb + (1.0 - lb) * jnp.where(pos, big, small))
        k_ref[:, cs] = (1.0 - lb) * jnp.where(pos, small, big)
        q_ref[:, cs] = _silu(qr) * (HEAD_K ** -0.5)
        gs_ref[:, cs] = _silu(gr)


def _hproj_kernel(x_ref, gpre_ref, win_ref, lbl_ref, q_ref, k_ref, lf_ref, v_ref, gs_ref):
    h = _rms(x_ref[...], gpre_ref[...]).astype(_BF16)
    _hgrn_gates(h, win_ref, lbl_ref, q_ref, k_ref, lf_ref, v_ref, gs_ref)


def _hgrn_proj(x2d, gpre, win_bf16, lb_logits, *, rows):
    n = x2d.shape[0]
    row_spec = pl.BlockSpec((rows, D_MODEL), lambda i: (i, 0))
    out = jax.ShapeDtypeStruct((n, D_MODEL), _F32)
    return pl.pallas_call(
        _hproj_kernel,
        grid=(n // rows,),
        in_specs=[row_spec, _const_spec((1, D_MODEL)), _const_spec(win_bf16.shape),
                  _const_spec(lb_logits.shape)],
        out_specs=[row_spec] * 5,
        out_shape=[out] * 5,
        compiler_params=pltpu.CompilerParams(
            dimension_semantics=("arbitrary",), vmem_limit_bytes=VMEM_LIMIT),
        name="hgrn_proj",
    )(x2d, gpre, win_bf16, lb_logits)


def _hout_kernel(x_ref, o_ref, gs_ref, gn_ref, wout_ref, gpost_ref, y_ref, zbuf):
    for hd in range(N_HEADS):
        hs = slice(hd * HEAD_V, (hd + 1) * HEAD_V)
        zbuf[:, hs] = (_rms(o_ref[:, hs], gn_ref[...]) * gs_ref[:, hs]).astype(_BF16)
    m = _dot(zbuf[...], wout_ref[...])
    y_ref[...] = x_ref[...] + _rms(m, gpost_ref[...])


def _hgrn_out(x2d, o2d, gs2d, gnorm, wout_bf16, gpost, *, rows):
    n = x2d.shape[0]
    row_spec = pl.BlockSpec((rows, D_MODEL), lambda i: (i, 0))
    return pl.pallas_call(
        _hout_kernel,
        grid=(n // rows,),
        in_specs=[row_spec, row_spec, row_spec, _const_spec((1, HEAD_V)),
                  _const_spec(wout_bf16.shape), _const_spec((1, D_MODEL))],
        out_specs=row_spec,
        out_shape=jax.ShapeDtypeStruct((n, D_MODEL), _F32),
        scratch_shapes=[pltpu.VMEM((rows, V_DIM), _BF16)],
        compiler_params=pltpu.CompilerParams(
            dimension_semantics=("arbitrary",), vmem_limit_bytes=VMEM_LIMIT),
        name="hgrn_out",
    )(x2d, o2d, gs2d, gnorm, wout_bf16, gpost)


def _gla_exponent_matrix():
    c = GLA_CHUNK
    t = np.arange(c)[:, None]
    w = np.arange(c)[None, :]
    blocks = [(w <= t)]
    for lev in range(1, GLA_FINE_LEVELS + 1):
        n = 1 << lev
        mid = (t // n) * n + n // 2 - 1
        later = (t > mid) & (w > mid) & (w <= t)
        earlier = (t <= mid) & (w > t) & (w <= mid)
        blocks.append(later | earlier)
    return np.concatenate(blocks, axis=0).astype(np.float32)


def _hgrn_prompt_kernel(*refs, rows, n_tiles, n_cast):
    x_ref, gpre_ref, win_ref, lbl_ref, mall_ref, gn_ref, wout_ref, gpost_ref = refs[:8]
    y_ref, sn_ref = refs[8 + n_cast:10 + n_cast]
    st, q_buf, k_buf, lf_buf, v_buf, gs_buf, z_buf = refs[-7:]
    _cast_blocks(refs[8:8 + n_cast], refs[10 + n_cast:10 + 2 * n_cast])
    c = GLA_CHUNK
    l = pl.program_id(1)

    @pl.when(l == 0)
    def _():
        st[...] = jnp.zeros(st.shape, _F32)

    x = x_ref[0]
    _hgrn_gates(_rms(x, gpre_ref[...]).astype(_BF16), win_ref, lbl_ref,
                q_buf, k_buf, lf_buf, v_buf, gs_buf)

    ti = lax.broadcasted_iota(jnp.int32, (c, c), 0)
    si = lax.broadcasted_iota(jnp.int32, (c, c), 1)
    ri = lax.broadcasted_iota(jnp.int32, (c, 1), 0)
    diag = ti == si
    lvl_mask, lvl_role = [], []
    for lev in range(1, GLA_LEVELS + 1):
        same = (ti >> lev) == (si >> lev)
        t_late = ((ti >> (lev - 1)) & 1) == 1
        s_early = ((si >> (lev - 1)) & 1) == 0
        lvl_mask.append(same & t_late & s_early)
        lvl_role.append(((ri >> (lev - 1)) & 1) == 1)

    def chunk(ci, carry):
        rs = pl.ds(pl.multiple_of(ci * c, c), c)
        lf = lf_buf[rs, :] * LOG2_E
        lf_hi = lf.astype(_BF16)
        r1 = lf - lf_hi.astype(_F32)
        lf_mid = r1.astype(_BF16)
        lf_lo = (r1 - lf_mid.astype(_F32)).astype(_BF16)
        cum = _dot(mall_ref[...], jnp.concatenate([lf_hi, lf_mid, lf_lo], axis=0))
        b = cum[0:c]
        eb = jnp.exp2(b)
        eb_rest = jnp.exp2(b[c - 1:c] - b)
        eb_last = eb[c - 1:c]
        q = q_buf[rs, :]
        k = k_buf[rs, :]
        vb = v_buf[rs, :]
        qd = (q * eb).astype(_BF16)
        kd = (k * eb_rest).astype(_BF16)
        lvl_x = []
        for lev in range(1, GLA_LEVELS + 1):
            n = 1 << lev
            if lev <= GLA_FINE_LEVELS:
                src = jnp.where(lvl_role[lev - 1], q, k)
                expo = cum[lev * c:(lev + 1) * c]
            else:
                src, expo = [], []
                for r0 in range(0, c, n):
                    mid = r0 + n // 2 - 1
                    src += [k[r0:mid + 1], q[mid + 1:r0 + n]]
                    expo += [b[mid:mid + 1] - b[r0:mid + 1], b[mid + 1:r0 + n] - b[mid:mid + 1]]
                src = jnp.concatenate(src, axis=0)
                expo = jnp.concatenate(expo, axis=0)
            lvl_x.append((src * jnp.exp2(expo)).astype(_BF16))
        qk = q * k
        heads = [slice(hd * HEAD_K, (hd + 1) * HEAD_K) for hd in range(N_HEADS)]
        scores = []
        for hs in heads:
            a = jnp.where(diag, jnp.sum(qk[:, hs], axis=-1, keepdims=True), 0.0)
            for lev in range(GLA_LEVELS):
                xl = lvl_x[lev][:, hs]
                a = jnp.where(lvl_mask[lev], _dot_nt(xl, xl), a)
            scores.append(a.astype(_BF16))
        s_old = [st[hd] for hd in range(N_HEADS)]
        for hd, hs in enumerate(heads):
            st[hd] = s_old[hd] * eb_last[:, hs] + _dot_tn(vb[:, hs], kd[:, hs])
        for hd, hs in enumerate(heads):
            o = _dot_nt(qd[:, hs], s_old[hd].astype(_BF16)) + _dot(scores[hd], vb[:, hs])
            z_buf[rs, hs] = (_rms(o, gn_ref[...]) * gs_buf[rs, hs]).astype(_BF16)
        return carry

    lax.fori_loop(0, rows // c, chunk, 0, unroll=GLA_UNROLL)

    m = _dot(z_buf[...], wout_ref[...])
    y_ref[0] = x + _rms(m, gpost_ref[...])

    @pl.when(l == n_tiles - 1)
    def _():
        for hd in range(N_HEADS):
            sn_ref[0, hd] = st[hd].T


def _hgrn_prompt(x, gpre, win_bf16, lb_logits, gnorm, wout_bf16, gpost, *, rows, cast=()):
    nb, n, _ = x.shape
    n_tiles = n // rows
    mall = jnp.asarray(np.tile(_gla_exponent_matrix(), (1, 3)), dtype=_BF16)
    row_spec = pl.BlockSpec((1, rows, D_MODEL), lambda b, l: (b, l, 0))
    st_spec = pl.BlockSpec((1, N_HEADS, HEAD_K, HEAD_V), lambda b, l: (b, 0, 0, 0))
    cast_in, cast_out, cast_shapes = _cast_specs(cast, (nb, n_tiles))
    kern = functools.partial(_hgrn_prompt_kernel, rows=rows, n_tiles=n_tiles, n_cast=len(cast))
    return pl.pallas_call(
        kern,
        grid=(nb, n_tiles),
        in_specs=[row_spec, _const_spec((1, D_MODEL)), _const_spec(win_bf16.shape),
                  _const_spec(lb_logits.shape), _const_spec(mall.shape), _const_spec((1, HEAD_V)),
                  _const_spec(wout_bf16.shape), _const_spec((1, D_MODEL))] + cast_in,
        out_specs=[row_spec, st_spec] + cast_out,
        out_shape=[jax.ShapeDtypeStruct(x.shape, _F32),
                   jax.ShapeDtypeStruct((nb, N_HEADS, HEAD_K, HEAD_V), _F32)] + cast_shapes,
        scratch_shapes=[
            pltpu.VMEM((N_HEADS, HEAD_V, HEAD_K), _F32),
            pltpu.VMEM((rows, F_DIM), _F32),
            pltpu.VMEM((rows, F_DIM), _F32),
            pltpu.VMEM((rows, F_DIM), _F32),
            pltpu.VMEM((rows, V_DIM), _BF16),
            pltpu.VMEM((rows, V_DIM), _F32),
            pltpu.VMEM((rows, V_DIM), _BF16),
        ],
        compiler_params=_params(),
        name="hgrn_prompt",
    )(x, gpre, win_bf16, lb_logits, mall, gnorm, wout_bf16, gpost, *(w for w, _ in cast))


def _gla_sample_kernel(q_ref, k_ref, lf_ref, v_ref, s0_ref, o_ref, sn_ref, *, steps):
    heads = [slice(hd * HEAD_K, (hd + 1) * HEAD_K) for hd in range(N_HEADS)]
    q = [q_ref[t] for t in range(steps)]
    k = [k_ref[t] for t in range(steps)]
    v = [v_ref[t] for t in range(steps)]
    b = [lf_ref[0]]
    for t in range(1, steps):
        b.append(b[-1] + lf_ref[t])
    b_last = b[-1]
    eb_last = jnp.exp(b_last)
    qd = [q[t] * jnp.exp(b[t]) for t in range(steps)]
    kd = [k[s] * jnp.exp(b_last - b[s]) for s in range(steps)]

    o_intra = []
    for t in range(steps):
        acc = None
        for s in range(t + 1):
            p = q[t] * k[s] if s == t else q[t] * k[s] * jnp.exp(b[t] - b[s])
            part = jnp.concatenate(
                [jnp.sum(p[:, hs], axis=-1, keepdims=True) * v[s][:, hs] for hs in heads], axis=-1)
            acc = part if acc is None else acc + part
        o_intra.append(acc)

    zero = jnp.zeros_like(b_last)
    rows_pad = [zero] * (SUBLANES - steps)
    e_hi = eb_last.astype(_BF16).astype(_F32)
    e_r = eb_last - e_hi
    e_mid = e_r.astype(_BF16).astype(_F32)
    e_lo = e_r - e_mid
    assert steps + 3 <= SUBLANES
    qd_b = jnp.stack(qd + rows_pad).swapaxes(0, 1).astype(_BF16)
    v_b = jnp.stack(v + rows_pad).swapaxes(0, 1)
    kd_b = jnp.stack(kd + [e_hi, e_mid, e_lo] + rows_pad[3:]).swapaxes(0, 1).astype(_BF16)
    ri = lax.broadcasted_iota(jnp.int32, (SUBLANES, HEAD_V), 0)
    ones_rows = ((ri >= steps) & (ri < steps + 3)).astype(_F32)

    o_inter = []
    for bi in range(SUBLANES):
        outs = []
        for hd, hs in enumerate(heads):
            s0 = s0_ref[bi, hd]
            outs.append(_dot(qd_b[bi][:, hs], s0.astype(_BF16)))
            rhs = jnp.concatenate([v_b[bi][:, hs], ones_rows], axis=-1).astype(_BF16)
            both = _dot_tn(kd_b[bi][:, hs], rhs)
            sn_ref[bi, hd] = both[:, HEAD_V:] * s0 + both[:, :HEAD_V]
        o_inter.append(jnp.concatenate(outs, axis=-1))
    o_inter = jnp.stack(o_inter).swapaxes(0, 1)
    for t in range(steps):
        o_ref[t] = o_intra[t] + o_inter[t]


def _gla_sample(q, k, lf, v, s0):
    steps, nb, _ = q.shape
    row_spec = pl.BlockSpec((steps, SUBLANES, D_MODEL), lambda i: (0, i, 0))
    st_spec = pl.BlockSpec((SUBLANES, N_HEADS, HEAD_K, HEAD_V), lambda i: (i, 0, 0, 0))
    kern = functools.partial(_gla_sample_kernel, steps=steps)
    return pl.pallas_call(
        kern,
        grid=(nb // SUBLANES,),
        in_specs=[row_spec, row_spec, row_spec, row_spec, st_spec],
        out_specs=[row_spec, st_spec],
        out_shape=[jax.ShapeDtypeStruct(q.shape, _F32), jax.ShapeDtypeStruct(s0.shape, _F32)],
        compiler_params=pltpu.CompilerParams(
            dimension_semantics=("arbitrary",), vmem_limit_bytes=VMEM_LIMIT),
        name="gla_sample",
    )(q, k, lf, v, s0)


def _row(v):
    return v.reshape(1, -1)


def _trunk_prompt(x, wts):
    nb, n, d = x.shape
    pool_ctx = jnp.zeros((nb, POOL_CTX, d), _F32)
    ffn_ctx = jnp.zeros((nb, CONV_W - 1, 2 * D_FF), _F32)

    f32 = wts["f32"]
    x, pool_st, w_up0, w_down0 = _pool_layer(
        x, pool_ctx, wts["mix_pre"][0], wts["pool_w"], wts["pool_scale"], wts["mix_post"][0],
        rows=POOL_TILE, step=1, pos0=0, cast=[(f32["w_up"], 0), (f32["w_down"], 0)])
    x, ffn_st0, w_in, w_out = _ffn_layer(
        x, ffn_ctx, wts["ffn_pre"][0], w_up0, wts["conv_w"][0], wts["conv_b"][0], w_down0,
        wts["ffn_post"][0], rows=FFN_TILE, step=1, fold=True,
        cast=[(f32["w_in"], 0), (f32["w_out"], 0)])
    x, s_new, w_up1, w_down1 = _hgrn_prompt(
        x, wts["mix_pre"][1], w_in, wts["lb_logits"], wts["gnorm"], w_out, wts["mix_post"][1],
        rows=ROW_TILE, cast=[(f32["w_up"], 1), (f32["w_down"], 1)])
    wts.update(w_in=w_in, w_out=w_out, w_up=[w_up0, w_up1], w_down=[w_down0, w_down1])
    x, ffn_st1 = _ffn_layer(x, ffn_ctx, wts["ffn_pre"][1], wts["w_up"][1],
                            wts["conv_w"][1], wts["conv_b"][1], wts["w_down"][1],
                            wts["ffn_post"][1], rows=FFN_TILE, step=1, fold=True)
    return x, pool_st[None], s_new[None], jnp.stack([ffn_st0, ffn_st1])


def _trunk_sample(x, pos0, state_pool, state_hgrn, state_ffn, wts):
    nb, steps, d = x.shape
    n = nb * steps

    def to_time_major(a):
        return a.transpose(1, 0, 2).reshape(1, a.shape[1] * nb, a.shape[2])

    def to_batch_major(a, t):
        return a.reshape(t, nb, a.shape[-1]).transpose(1, 0, 2)

    xt = to_time_major(x)
    xt, pool_st = _pool_layer(xt, to_time_major(state_pool[0]), wts["mix_pre"][0], wts["pool_w"],
                              wts["pool_scale"], wts["mix_post"][0], rows=n, step=nb, pos0=pos0)
    xt, ffn_st0 = _ffn_stream_layer(xt, to_time_major(state_ffn[0]), wts["ffn_pre"][0],
                                    wts["w_up"][0], wts["conv_w"][0], wts["conv_b"][0],
                                    wts["w_down"][0], wts["ffn_post"][0], step=nb)
    x2d = xt.reshape(n, d)
    proj = _hgrn_proj(x2d, wts["mix_pre"][1], wts["w_in"], wts["lb_logits"], rows=n)

    q, k, lf, v = (a.reshape(steps, nb, d) for a in proj[:4])
    o, s_new = _gla_sample(q, k, lf, v, state_hgrn[0])
    x2d = _hgrn_out(x2d, o.reshape(n, d), proj[4], wts["gnorm"], wts["w_out"],
                    wts["mix_post"][1], rows=n)
    xt, ffn_st1 = _ffn_stream_layer(x2d.reshape(1, n, d), to_time_major(state_ffn[1]),
                                    wts["ffn_pre"][1], wts["w_up"][1], wts["conv_w"][1],
                                    wts["conv_b"][1], wts["w_down"][1], wts["ffn_post"][1],
                                    step=nb)
    y = to_batch_major(xt, steps)
    new_pool = to_batch_major(pool_st, POOL_CTX)[None]
    new_ffn = jnp.stack([ffn_st0, ffn_st1]).reshape(2, CONV_W - 1, nb, 2 * D_FF).transpose(0, 2, 1, 3)
    return y, new_pool, s_new[None], new_ffn


def kernel(x_prompt, x_sample, state_pool, state_hgrn, state_ffn_conv, norm_mix_pre, norm_mix_post,
           norm_ffn_pre, norm_ffn_post, pool_w, pool_scale, hgrn_w_in, hgrn_lb_logits, hgrn_gnorm,
           hgrn_w_out, ffn_w_up, ffn_conv_w, ffn_conv_b, ffn_w_down):
    depth = ffn_w_up.shape[0]
    assert depth == 2 and pool_w.shape[0] == 1 and hgrn_w_in.shape[0] == 1
    wts = {
        "mix_pre": [_row(norm_mix_pre[i]) for i in range(depth)],
        "mix_post": [_row(norm_mix_post[i]) for i in range(depth)],
        "ffn_pre": [_row(norm_ffn_pre[i]) for i in range(depth)],
        "ffn_post": [_row(norm_ffn_post[i]) for i in range(depth)],
        "pool_w": pool_w[0].astype(_BF16),
        "pool_scale": _row(pool_scale[0]),
        "lb_logits": hgrn_lb_logits,
        "gnorm": _row(hgrn_gnorm[0]),
        "conv_w": [ffn_conv_w[i] for i in range(depth)],
        "conv_b": [_row(ffn_conv_b[i]) for i in range(depth)],
        "f32": {"w_in": hgrn_w_in, "w_out": hgrn_w_out, "w_up": ffn_w_up, "w_down": ffn_w_down},
    }
    y_p, pool_p, hgrn_p, ffn_p = _trunk_prompt(x_prompt, wts)
    y_s, pool_s, hgrn_s, ffn_s = _trunk_sample(x_sample, PAST_LEN, state_pool, state_hgrn,
                                               state_ffn_conv, wts)
    return (y_p, y_s, pool_p, pool_s, hgrn_p, hgrn_s, ffn_p, ffn_s)
```
